```python
import jax
import jax.numpy as jnp
from jax import lax
import numpy as np

D_MODEL = 4096
BATCH = 1
SEQ = 8192
DEPTH = 1
DEC_BATCH = 128
DEC_SEQ = 8
PAST_LEN = 8192
PAGE_SIZE = 128

HEAD_DIM = 64
N_Q_HEADS = D_MODEL // HEAD_DIM
N_KV_HEADS = 8
GROUP = N_Q_HEADS // N_KV_HEADS
ROT_DIM = HEAD_DIM // 4
ROPE_THETA = 500000.0
WINDOW = 128
ATT_BLOCK = 128
NEG_INF = -1e30
D_CONV = D_MODEL
CONV_W = 3
N_EXPERTS = 32
TOP_K = 4
D_FF = D_MODEL
SWIGLU_LIMIT = 7.0
SWIGLU_ALPHA = 1.702
MOE_BLOCK = 128
PLE_DIM = 256
RMS_EPS = 1e-5

Q_W = N_Q_HEADS * HEAD_DIM
KV_W = N_KV_HEADS * HEAD_DIM
N_IN = Q_W + 2 * KV_W + 3 * D_CONV + 2 * D_MODEL
SPLIT_POINTS = (Q_W, Q_W + KV_W, Q_W + 2 * KV_W, Q_W + 2 * KV_W + D_CONV,
                Q_W + 2 * KV_W + 2 * D_CONV, Q_W + 2 * KV_W + 3 * D_CONV,
                Q_W + 2 * KV_W + 3 * D_CONV + D_MODEL)

kernel_name = "hybrid_conv_swa_moe_step"


def rmsnorm(x, g):
    xf = x.astype(jnp.float32)
    xf = xf * lax.rsqrt(jnp.mean(xf * xf, axis=-1, keepdims=True) + RMS_EPS)
    return xf.astype(x.dtype) * g


def partial_rope(x, pos):
    half = ROT_DIM // 2
    inv_freq = ROPE_THETA ** (-jnp.arange(half, dtype=jnp.float32) / half)
    ang = pos.astype(jnp.float32)[:, None] * inv_freq[None, :]
    cos = jnp.cos(ang)[None, :, None, :]
    sin = jnp.sin(ang)[None, :, None, :]
    xr = x[..., :ROT_DIM].astype(jnp.float32)
    x1, x2 = xr[..., :half], xr[..., half:]
    rot = jnp.concatenate([x1 * cos - x2 * sin, x2 * cos + x1 * sin], axis=-1)
    return jnp.concatenate([rot.astype(x.dtype), x[..., ROT_DIM:]], axis=-1)


def sink_attend(q, k, v, mask, sinks):
    s = jnp.einsum("...qkgd,...skd->...kgqs", q, k,
                   preferred_element_type=jnp.float32) * (HEAD_DIM ** -0.5)
    s = jnp.where(mask[..., None, None, :, :], s, NEG_INF)
    sk = sinks.astype(jnp.float32)[:, :, None]
    m = jnp.maximum(jnp.max(s, axis=-1), sk)
    pr = jnp.exp(s - m[..., None])
    den = jnp.sum(pr, axis=-1) + jnp.exp(sk - m)
    pr = pr / den[..., None]
    return jnp.einsum("...kgqs,...skd->...qkgd", pr.astype(v.dtype), v)


def window_attn_prompt(q, k, v, sinks):
    B, T = q.shape[0], q.shape[1]
    nb = T // ATT_BLOCK
    qb = q.reshape(B, nb, ATT_BLOCK, N_KV_HEADS, GROUP, HEAD_DIM)

    def band(t):
        tp = jnp.concatenate([jnp.zeros_like(t[:, :ATT_BLOCK]), t], axis=1)
        tb = tp.reshape(B, nb + 1, ATT_BLOCK, N_KV_HEADS, HEAD_DIM)
        return jnp.concatenate([tb[:, :-1], tb[:, 1:]], axis=2)

    qi = jnp.arange(ATT_BLOCK)[:, None]
    kj = jnp.arange(2 * ATT_BLOCK)[None, :] - ATT_BLOCK
    blk0 = (jnp.arange(nb) * ATT_BLOCK)[:, None, None]
    mask = (kj <= qi) & (kj > qi - WINDOW) & (blk0 + kj >= 0)
    out = sink_attend(qb, band(k), band(v), mask, sinks)
    w = min(WINDOW, T)
    return out.reshape(B, T, Q_W), k[:, T - w:], v[:, T - w:]


def window_attn_sample(q, k, v, k_buf, v_buf, sinks):
    S = q.shape[1]
    W = k_buf.shape[1]
    kk = jnp.concatenate([k_buf, k], axis=1)
    vv = jnp.concatenate([v_buf, v], axis=1)
    q_pos = PAST_LEN + jnp.arange(S)
    k_pos = PAST_LEN - W + jnp.arange(W + S)
    mask = (k_pos[None, :] <= q_pos[:, None]) & (k_pos[None, :] > q_pos[:, None] - WINDOW)
    out = sink_attend(q, kk, vv, mask, sinks)
    w_new = min(WINDOW, PAST_LEN + S)
    return out.reshape(q.shape[0], S, Q_W), kk[:, W + S - w_new:], vv[:, W + S - w_new:]


def short_conv(u, prev, w):
    T = u.shape[1]
    up = jnp.concatenate([prev, u], axis=1)
    y = w[0] * up[:, 0:T]
    for j in range(1, CONV_W):
        y = y + w[j] * up[:, j:j + T]
    return y, up[:, T:]


def token_mixers(x, pos, k_buf, v_buf, conv_prev, g_mix, w_in, b_in, conv_w, sinks, w_out, b_out):
    B, T = x.shape[0], x.shape[1]
    u = rmsnorm(x, g_mix)
    z = u @ w_in + b_in
    q, k, v, xc, b_gate, c_gate, g_att, g_conv = jnp.split(z, SPLIT_POINTS, axis=-1)
    q = partial_rope(q.reshape(B, T, N_Q_HEADS, HEAD_DIM), pos)
    q = q.reshape(B, T, N_KV_HEADS, GROUP, HEAD_DIM)
    k = partial_rope(k.reshape(B, T, N_KV_HEADS, HEAD_DIM), pos)
    v = v.reshape(B, T, N_KV_HEADS, HEAD_DIM)
    sk = sinks.reshape(N_KV_HEADS, GROUP)
    if k_buf is None:
        y_att, k_st, v_st = window_attn_prompt(q, k, v, sk)
    else:
        y_att, k_st, v_st = window_attn_sample(q, k, v, k_buf, v_buf, sk)
    y_conv, conv_st = short_conv(c_gate * xc, conv_prev, conv_w)
    y_conv = b_gate * y_conv
    merged = jax.nn.sigmoid(g_att) * y_att + jax.nn.sigmoid(g_conv) * y_conv
    return merged @ w_out + b_out, k_st, v_st, conv_st


def moe_ffn(h, w_router, b_router, w_gate_up, b_gate_up, w_down, b_down, layer):
    shp = h.shape
    hf = h.reshape(-1, D_MODEL)
    T = hf.shape[0]
    A = T * TOP_K
    logits = jnp.matmul(hf, w_router, preferred_element_type=jnp.float32) + b_router.astype(jnp.float32)
    top_v, top_i = lax.top_k(logits, TOP_K)
    gates = jax.nn.softmax(top_v, axis=-1)
    flat_e = top_i.reshape(-1).astype(jnp.int32)
    flat_t = jnp.repeat(jnp.arange(T, dtype=jnp.int32), TOP_K)
    flat_g = gates.reshape(-1)
    se, st, sg = lax.sort((flat_e, flat_t, flat_g), num_keys=1, is_stable=True)
    counts = jnp.bincount(flat_e, length=N_EXPERTS)
    padded = ((counts + MOE_BLOCK - 1) // MOE_BLOCK) * MOE_BLOCK
    pad_end = jnp.cumsum(padded)
    pad_start = pad_end - padded
    raw_start = jnp.cumsum(counts) - counts
    dest = pad_start[se] + jnp.arange(A, dtype=jnp.int32) - raw_start[se]
    n_blocks = -(-A // MOE_BLOCK) + N_EXPERTS
    cap = n_blocks * MOE_BLOCK
    tok_buf = jnp.zeros((cap,), jnp.int32).at[dest].set(st)
    gate_buf = jnp.zeros((cap,), hf.dtype).at[dest].set(sg.astype(hf.dtype))
    blk_start = jnp.arange(n_blocks, dtype=jnp.int32) * MOE_BLOCK
    blk_e = jnp.minimum(jnp.searchsorted(pad_end, blk_start, side="right"), N_EXPERTS - 1).astype(jnp.int32)

    def body(y, blk):
        tok, g, e = blk
        xb = hf[tok]
        gu = xb @ w_gate_up[layer, e] + b_gate_up[layer, e]
        x_glu = jnp.minimum(gu[:, 0::2], SWIGLU_LIMIT)
        x_lin = jnp.clip(gu[:, 1::2], -SWIGLU_LIMIT, SWIGLU_LIMIT)
        act = x_glu * jax.nn.sigmoid(SWIGLU_ALPHA * x_glu) * (x_lin + 1.0)
        out = (act @ w_down[layer, e] + b_down[layer, e]) * g[:, None]
        return y.at[tok].add(out), None

    y, _ = lax.scan(body, jnp.zeros_like(hf),
                    (tok_buf.reshape(n_blocks, MOE_BLOCK), gate_buf.reshape(n_blocks, MOE_BLOCK), blk_e))
    return y.reshape(shp)


def decoder_layer(h, p, pos, k_buf, v_buf, conv_prev, i, params):
    (g_mix, w_in, b_in, conv_w, sinks, w_out, b_out, g_ffn, w_router, b_router,
     w_gate_up, b_gate_up, w_down, b_down, g_ple, w_ple_gate, w_ple) = params
    y_mix, k_st, v_st, conv_st = token_mixers(h, pos, k_buf, v_buf, conv_prev, g_mix[i], w_in[i], b_in[i],
                                              conv_w[i], sinks[i], w_out[i], b_out[i])
    h = h + y_mix
    h = h + moe_ffn(rmsnorm(h, g_ffn[i]), w_router[i], b_router[i], w_gate_up, b_gate_up, w_down, b_down, i)
    gate = jax.nn.sigmoid(rmsnorm(h, g_ple[i]) @ w_ple_gate[i])
    h = h + gate * (p @ w_ple[i])
    return h, k_st, v_st, conv_st


def setup_inputs(seed: int = 0) -> dict:
    key = jax.random.key(seed)
    ks = jax.random.split(key, 25)
    f32 = jnp.float32
    D = D_MODEL
    win = min(WINDOW, PAST_LEN)

    def nrm(k, shape, scale):
        return jax.random.normal(k, shape, f32) * scale

    return {
        "x_prompt": nrm(ks[0], (BATCH, SEQ, D), 1.0),
        "x_sample": nrm(ks[1], (DEC_BATCH, DEC_SEQ, D), 1.0),
        "cache_k": nrm(ks[2], (DEPTH, DEC_BATCH, win, N_KV_HEADS, HEAD_DIM), 1.0),
        "cache_v": nrm(ks[3], (DEPTH, DEC_BATCH, win, N_KV_HEADS, HEAD_DIM), 1.0),
        "state_conv": nrm(ks[4], (DEPTH, DEC_BATCH, CONV_W - 1, D_CONV), 1.0),
        "p_prompt": nrm(ks[5], (DEPTH, BATCH, SEQ, PLE_DIM), 1.0),
        "p_sample": nrm(ks[6], (DEPTH, DEC_BATCH, DEC_SEQ, PLE_DIM), 1.0),
        "g_mix": 1.0 + nrm(ks[7], (DEPTH, D), 0.05),
        "w_in": nrm(ks[8], (DEPTH, D, N_IN), D ** -0.5),
        "b_in": nrm(ks[9], (DEPTH, N_IN), 0.02),
        "conv_w": nrm(ks[10], (DEPTH, CONV_W, D_CONV), CONV_W ** -0.5),
        "sinks": nrm(ks[11], (DEPTH, N_Q_HEADS), 0.5),
        "w_out": nrm(ks[12], (DEPTH, D, D), D ** -0.5),
        "b_out": nrm(ks[13], (DEPTH, D), 0.02),
        "g_ffn": 1.0 + nrm(ks[14], (DEPTH, D), 0.05),
        "w_router": nrm(ks[15], (DEPTH, D, N_EXPERTS), D ** -0.5),
        "b_router": nrm(ks[16], (DEPTH, N_EXPERTS), 0.01),
        "w_gate_up": nrm(ks[17], (DEPTH, N_EXPERTS, D, 2 * D_FF), D ** -0.5),
        "b_gate_up": nrm(ks[18], (DEPTH, N_EXPERTS, 2 * D_FF), 0.02),
        "w_down": nrm(ks[19], (DEPTH, N_EXPERTS, D_FF, D), D_FF ** -0.5),
        "b_down": nrm(ks[20], (DEPTH, N_EXPERTS, D), 0.02),
        "g_ple": 1.0 + nrm(ks[21], (DEPTH, D), 0.05),
        "w_ple_gate": nrm(ks[22], (DEPTH, D, D), D ** -0.5),
        "w_ple": nrm(ks[23], (DEPTH, PLE_DIM, D), PLE_DIM ** -0.5),
        "g_final": 1.0 + nrm(ks[24], (D,), 0.05),
    }


def reference(x_prompt, x_sample, cache_k, cache_v, state_conv, p_prompt, p_sample,
              g_mix, w_in, b_in, conv_w, sinks, w_out, b_out, g_ffn, w_router, b_router,
              w_gate_up, b_gate_up, w_down, b_down, g_ple, w_ple_gate, w_ple, g_final):
    params = (g_mix, w_in, b_in, conv_w, sinks, w_out, b_out, g_ffn, w_router, b_router,
              w_gate_up, b_gate_up, w_down, b_down, g_ple, w_ple_gate, w_ple)
    pos_prompt = jnp.arange(x_prompt.shape[1])
    pos_sample = PAST_LEN + jnp.arange(x_sample.shape[1])
    hp, hs = x_prompt, x_sample
    kp, vp, cp, ks_, vs_, cs_ = [], [], [], [], [], []
    for i in range(DEPTH):
        zeros_conv = jnp.zeros((hp.shape[0], CONV_W - 1, D_CONV), hp.dtype)
        hp, k_i, v_i, c_i = decoder_layer(hp, p_prompt[i], pos_prompt, None, None, zeros_conv, i, params)
        kp.append(k_i)
        vp.append(v_i)
        cp.append(c_i)
        hs, k_i, v_i, c_i = decoder_layer(hs, p_sample[i], pos_sample, cache_k[i], cache_v[i],
                                          state_conv[i], i, params)
        ks_.append(k_i)
        vs_.append(v_i)
        cs_.append(c_i)
    y_prompt = rmsnorm(hp, g_final)
    y_sample = rmsnorm(hs, g_final)
    return (y_prompt, y_sample, jnp.stack(kp), jnp.stack(vp), jnp.stack(cp),
            jnp.stack(ks_), jnp.stack(vs_), jnp.stack(cs_))
```

```python
import functools

import jax
import jax.numpy as jnp
import numpy as np
from jax import lax
from jax.experimental import pallas as pl
from jax.experimental.pallas import tpu as pltpu

BF16 = jnp.bfloat16
F32 = jnp.float32
I32 = jnp.int32

D = 4096
T_P = 8192
N_SEQ = 128
S_LEN = 8
T_S = N_SEQ * S_LEN
T_ALL = T_P + T_S
PAST = 8192
HD = 64
N_KV = 8
KV_W = N_KV * HD
ROT = 16
THETA = 500000.0
WIN = 128
N_E = 32
TOP_K = 4
N_ASSIGN = T_ALL * TOP_K
LIMIT = 7.0
ALPHA = 1.702
PLE = 256
EPS = 1e-5
NEG = -1e30
N_IN = 25600

CB = 512
C_K, C_V, C_XC, C_B, C_C, C_GA, C_GC = 8, 9, 10, 18, 26, 34, 42

VMEM_LIMIT = 56 * 1024 * 1024

TM_E = 1280
NB_E = N_ASSIGN // TM_E + N_E + 1
CAP_E = NB_E * TM_E
GT = 256


def _cparams(n_axes):
    return pltpu.CompilerParams(dimension_semantics=("arbitrary",) * n_axes,
                                vmem_limit_bytes=VMEM_LIMIT)


def _rms_cast_kernel(x_ref, g_ref, o_ref):
    x = x_ref[...]
    ms = jnp.mean(x * x, axis=-1, keepdims=True)
    o_ref[...] = (x * lax.rsqrt(ms + EPS) * g_ref[...]).astype(o_ref.dtype)


def _rms_norm(x, g, out_dtype, row0=0, rows=None, tm=256):
    rows = x.shape[0] if rows is None else rows
    b0 = row0 // tm
    return pl.pallas_call(
        _rms_cast_kernel,
        grid=(rows // tm,),
        in_specs=[pl.BlockSpec((tm, D), lambda i: (i + b0, 0)),
                  pl.BlockSpec((1, D), lambda i: (0, 0))],
        out_specs=pl.BlockSpec((tm, D), lambda i: (i, 0)),
        out_shape=jax.ShapeDtypeStruct((rows, D), out_dtype),
        compiler_params=_cparams(1),
        name="rms_norm",
    )(x, g.reshape(1, D))


def _mm_bias_kernel(lhs_ref, w_ref, b_ref, o_ref):
    acc = jnp.dot(lhs_ref[...], w_ref[...].astype(BF16), preferred_element_type=F32)
    o_ref[...] = acc + b_ref[...]


def _mm_bias_res_kernel(lhs_ref, w_ref, b_ref, r_ref, o_ref):
    acc = jnp.dot(lhs_ref[...], w_ref[...].astype(BF16), preferred_element_type=F32)
    o_ref[...] = acc + b_ref[...] + r_ref[...]


def _mm_ple_kernel(lhs_ref, w_ref, h_ref, p_ref, wp_ref, o_ref):
    acc = jnp.dot(lhs_ref[...], w_ref[...].astype(BF16), preferred_element_type=F32)
    pp = jnp.dot(p_ref[...].astype(BF16), wp_ref[...].astype(BF16), preferred_element_type=F32)
    o_ref[...] = h_ref[...] + jax.nn.sigmoid(acc) * pp


def _dense(kernel, lhs, w, extras, extra_specs, tm, tn, name):
    M, K = lhs.shape
    N = w.shape[1]
    return pl.pallas_call(
        kernel,
        grid=(M // tm, N // tn),
        in_specs=[pl.BlockSpec((tm, K), lambda i, j: (i, 0)),
                  pl.BlockSpec((K, tn), lambda i, j: (0, j))] + extra_specs,
        out_specs=pl.BlockSpec((tm, tn), lambda i, j: (i, j)),
        out_shape=jax.ShapeDtypeStruct((M, N), F32),
        compiler_params=_cparams(2),
        name=name,
    )(lhs, w, *extras)


def _rope(x, c, s1, s2):
    w = x.shape[1]
    reps = w // 128
    ct = jnp.concatenate([c] * reps, axis=1) if reps > 1 else c
    s1t = jnp.concatenate([s1] * reps, axis=1) if reps > 1 else s1
    s2t = jnp.concatenate([s2] * reps, axis=1) if reps > 1 else s2
    return x * ct + pltpu.roll(x, 8, 1) * s1t + pltpu.roll(x, w - 8, 1) * s2t


def _half_placed(chunk, half):
    lane = lax.broadcasted_iota(I32, chunk.shape, 1)
    low = lane < HD
    other = pltpu.roll(chunk, HD, 1)
    zero = jnp.zeros_like(chunk)
    first = half == 0
    lo = jnp.where(low, jnp.where(first, chunk, other), zero)
    hi = jnp.where(low, zero, jnp.where(first, other, chunk))
    return lo.astype(BF16), hi.astype(BF16)


def _attend_group(g, q_rot, keys, vals, mask, sink_ref, o_ref):
    k_lo, k_hi = _half_placed(keys, g % 2)
    v_lo, v_hi = _half_placed(vals, g % 2)
    for i in range(4):
        col = 128 * i
        qp = (q_rot[:, col:col + 128] * (HD ** -0.5)).astype(BF16)
        out = None
        for par, (k2, v2) in enumerate(((k_lo, v_lo), (k_hi, v_hi))):
            sink = sink_ref[8 * g + 2 * i + par]
            s = lax.dot_general(qp, k2, (((1,), (1,)), ((), ())), preferred_element_type=F32)
            s = jnp.where(mask, s, NEG)
            m = jnp.maximum(jnp.max(s, axis=-1, keepdims=True), sink)
            p = jnp.exp(s - m)
            den = jnp.sum(p, axis=-1, keepdims=True) + jnp.exp(sink - m)
            o = jnp.dot(p.astype(BF16), v2, preferred_element_type=F32) / den
            out = o if out is None else out + o
        o_ref[:, col:col + 128] = out.astype(o_ref.dtype)


def _attn_prompt_kernel(sink_ref, q_ref, kc_ref, kp_ref, vc_ref, vp_ref,
                        cc_ref, s1c_ref, s2c_ref, cp_ref, s1p_ref, s2p_ref,
                        o_ref, krot_ref):
    b = pl.program_id(0)
    g = pl.program_id(1)
    q_rot = _rope(q_ref[...], cc_ref[...], s1c_ref[...], s2c_ref[...])
    k_cur = _rope(kc_ref[...], cc_ref[...], s1c_ref[...], s2c_ref[...])
    k_prev = _rope(kp_ref[...], cp_ref[...], s1p_ref[...], s2p_ref[...])
    krot_ref[...] = k_cur
    keys = jnp.concatenate([k_prev, k_cur], axis=0)
    vals = jnp.concatenate([vp_ref[...], vc_ref[...]], axis=0)
    r = lax.broadcasted_iota(I32, (128, 256), 0)
    c = lax.broadcasted_iota(I32, (128, 256), 1)
    mask = (c > r) & (c <= r + WIN) & ((b > 0) | (c >= 128))
    _attend_group(g, q_rot, keys, vals, mask, sink_ref, o_ref)


def _attn_sample_kernel(sink_ref, q_ref, kn_ref, vn_ref, ck_ref, cv_ref,
                        c_ref, s1_ref, s2_ref, mask_ref, o_ref, krot_ref):
    g = pl.program_id(1)
    q_rot = _rope(q_ref[...], c_ref[...], s1_ref[...], s2_ref[...])
    k_new = _rope(kn_ref[...], c_ref[...], s1_ref[...], s2_ref[...])
    krot_ref[...] = k_new
    keys = jnp.concatenate([ck_ref[...], k_new], axis=0)
    vals = jnp.concatenate([cv_ref[...], vn_ref[...]], axis=0)
    mask = mask_ref[...] > 0.5
    _attend_group(g, q_rot, keys, vals, mask, sink_ref, o_ref)


def _rope_tables(pos):
    half = ROT // 2
    inv_freq = THETA ** (-jnp.arange(half, dtype=F32) / half)
    ang = pos.astype(F32)[:, None] * inv_freq[None, :]
    cos, sin = jnp.cos(ang), jnp.sin(ang)
    n = pos.shape[0]
    ones = jnp.ones((n, HD - ROT), F32)
    zeros = jnp.zeros((n, HD - ROT), F32)
    z8 = jnp.zeros((n, half), F32)
    c = jnp.concatenate([cos, cos, ones], axis=1)
    s1 = jnp.concatenate([z8, sin, zeros], axis=1)
    s2 = jnp.concatenate([-sin, z8, zeros], axis=1)
    return tuple(jnp.concatenate([t, t], axis=1) for t in (c, s1, s2))


def _sample_mask():
    sb = 16
    r = np.arange(sb * S_LEN)
    sq, qi = r // S_LEN, r % S_LEN
    c = np.arange(sb * WIN)
    m_cache = (c[None, :] // WIN == sq[:, None]) & (c[None, :] % WIN > qi[:, None])
    cn = np.arange(sb * S_LEN)
    m_new = (cn[None, :] // S_LEN == sq[:, None]) & (cn[None, :] % S_LEN <= qi[:, None])
    return np.concatenate([m_cache, m_new], axis=1).astype(np.float32)


def _attention(z, cache_k, cache_v, sinks):
    smem = pl.BlockSpec(memory_space=pltpu.SMEM)
    tabs_p = _rope_tables(jnp.arange(T_P))
    prev = lambda b: jnp.maximum(b - 1, 0)
    ck0, cv0 = C_K * CB // 128, C_V * CB // 128
    tab_cur = pl.BlockSpec((128, 128), lambda b, g: (b, 0))
    tab_prev = pl.BlockSpec((128, 128), lambda b, g: (prev(b), 0))
    y_p, krot_p = pl.pallas_call(
        _attn_prompt_kernel,
        grid=(T_P // 128, N_KV),
        in_specs=[smem,
                  pl.BlockSpec((128, CB), lambda b, g: (b, g)),
                  pl.BlockSpec((128, 128), lambda b, g: (b, ck0 + g // 2)),
                  pl.BlockSpec((128, 128), lambda b, g: (prev(b), ck0 + g // 2)),
                  pl.BlockSpec((128, 128), lambda b, g: (b, cv0 + g // 2)),
                  pl.BlockSpec((128, 128), lambda b, g: (prev(b), cv0 + g // 2)),
                  tab_cur, tab_cur, tab_cur, tab_prev, tab_prev, tab_prev],
        out_specs=[pl.BlockSpec((128, CB), lambda b, g: (b, g)),
                   pl.BlockSpec((128, 128), lambda b, g: (b, g // 2))],
        out_shape=[jax.ShapeDtypeStruct((T_P, D), BF16),
                   jax.ShapeDtypeStruct((T_P, KV_W), F32)],
        compiler_params=_cparams(2),
        name="attn_prompt",
    )(sinks, z, z, z, z, z, *tabs_p, *tabs_p)

    sb = 16
    rows = sb * S_LEN
    tabs_s = _rope_tables(PAST + (jnp.arange(rows) % S_LEN))
    mask = jnp.asarray(_sample_mask())
    nk = sb * WIN + rows
    rb0 = T_P // rows
    const = lambda shape: pl.BlockSpec(shape, lambda b, g: (0, 0))
    y_s, krot_s = pl.pallas_call(
        _attn_sample_kernel,
        grid=(N_SEQ // sb, N_KV),
        in_specs=[smem,
                  pl.BlockSpec((rows, CB), lambda b, g: (b + rb0, g)),
                  pl.BlockSpec((rows, 128), lambda b, g: (b + rb0, ck0 + g // 2)),
                  pl.BlockSpec((rows, 128), lambda b, g: (b + rb0, cv0 + g // 2)),
                  pl.BlockSpec((sb * WIN, 128), lambda b, g: (b, g // 2)),
                  pl.BlockSpec((sb * WIN, 128), lambda b, g: (b, g // 2)),
                  const((rows, 128)), const((rows, 128)), const((rows, 128)),
                  const((rows, nk))],
        out_specs=[pl.BlockSpec((rows, CB), lambda b, g: (b, g)),
                   pl.BlockSpec((rows, 128), lambda b, g: (b, g // 2))],
        out_shape=[jax.ShapeDtypeStruct((T_S, D), BF16),
                   jax.ShapeDtypeStruct((T_S, KV_W), F32)],
        compiler_params=_cparams(2),
        name="attn_sample",
    )(sinks, z, z, z, cache_k.reshape(N_SEQ * WIN, KV_W), cache_v.reshape(N_SEQ * WIN, KV_W),
      *tabs_s, mask)
    return y_p, krot_p, y_s, krot_s


MT = 512


def _merge_kernel(ya_ref, xc_ref, bg_ref, cg_ref, ga_ref, gc_ref, xc8_ref, cg8_ref,
                  s1_ref, s2_ref, cw_ref, m_ref, cx_ref):
    i = pl.program_id(1)
    n_p = T_P // MT
    is_sample = i >= n_p
    cx = cg_ref[...] * xc_ref[...]
    cx_ref[...] = cx
    row = lax.broadcasted_iota(I32, cx.shape, 0)
    r1 = pltpu.roll(cx, 1, 0)
    r2 = pltpu.roll(cx, 2, 0)
    p8 = cg8_ref[...] * xc8_ref[...] * jnp.where(i > 0, 1.0, 0.0)
    pad = jnp.zeros((MT - 8, cx.shape[1]), F32)
    p1 = jnp.concatenate([pltpu.roll(p8, 1, 0), pad], axis=0)
    p2 = jnp.concatenate([pltpu.roll(p8, 2, 0), pad], axis=0)
    c1_p = jnp.where(row >= 1, r1, p1)
    c2_p = jnp.where(row >= 2, r2, p2)
    c1_s = jnp.where(row % S_LEN >= 1, r1, s1_ref[...])
    c2_s = jnp.where(row % S_LEN >= 2, r2, s2_ref[...])
    c1 = jnp.where(is_sample, c1_s, c1_p)
    c2 = jnp.where(is_sample, c2_s, c2_p)
    w = cw_ref[...]
    y_conv = bg_ref[...] * (w[0:1] * c2 + w[1:2] * c1 + w[2:3] * cx)
    merged = jax.nn.sigmoid(ga_ref[...]) * ya_ref[...].astype(F32) + jax.nn.sigmoid(gc_ref[...]) * y_conv
    m_ref[...] = merged.astype(m_ref.dtype)


def _merge(z, y_att, state_conv, conv_w):
    n_p = T_P // MT
    zb = lambda off: pl.BlockSpec((MT, MT), lambda j, i: (i, off + j))
    z8 = lambda off: pl.BlockSpec((8, MT), lambda j, i: (jnp.maximum(i * (MT // 8) - 1, 0), off + j))
    sblk = pl.BlockSpec((MT, MT), lambda j, i: (jnp.maximum(i - n_p, 0), j))
    st = state_conv
    zrow = jnp.zeros((N_SEQ, 1, D), F32)
    s1 = jnp.concatenate([st[:, 1:2], jnp.zeros((N_SEQ, S_LEN - 1, D), F32)], axis=1).reshape(T_S, D)
    s2 = jnp.concatenate([st[:, 0:1], st[:, 1:2], jnp.zeros((N_SEQ, S_LEN - 2, D), F32)], axis=1).reshape(T_S, D)
    del zrow
    cw = jnp.concatenate([conv_w, jnp.zeros((5, D), F32)], axis=0)
    return pl.pallas_call(
        _merge_kernel,
        grid=(D // MT, T_ALL // MT),
        in_specs=[pl.BlockSpec((MT, MT), lambda j, i: (i, j)),
                  zb(C_XC), zb(C_B), zb(C_C), zb(C_GA), zb(C_GC), z8(C_XC), z8(C_C),
                  sblk, sblk,
                  pl.BlockSpec((8, MT), lambda j, i: (0, j))],
        out_specs=[pl.BlockSpec((MT, MT), lambda j, i: (i, j)),
                   pl.BlockSpec((MT, MT), lambda j, i: (i, j))],
        out_shape=[jax.ShapeDtypeStruct((T_ALL, D), BF16),
                   jax.ShapeDtypeStruct((T_ALL, D), F32)],
        compiler_params=_cparams(2),
        name="merge",
    )(y_att, z, z, z, z, z, z, z, s1, s2, cw)


RT = 256


def _router_kernel(h_ref, g_ref, wr_ref, br_ref, hn_ref, ti_ref, tg_ref):
    x = h_ref[...]
    ms = jnp.mean(x * x, axis=-1, keepdims=True)
    hn = x * lax.rsqrt(ms + EPS) * g_ref[...]
    hn_ref[...] = hn
    w = wr_ref[...]
    h_hi = hn.astype(BF16)
    h_lo = (hn - h_hi.astype(F32)).astype(BF16)
    w_hi = w.astype(BF16)
    w_lo = (w - w_hi.astype(F32)).astype(BF16)
    logits = (jnp.dot(h_hi, w_hi, preferred_element_type=F32)
              + jnp.dot(h_hi, w_lo, preferred_element_type=F32)
              + jnp.dot(h_lo, w_hi, preferred_element_type=F32)) + br_ref[...]
    lane = lax.broadcasted_iota(I32, logits.shape, 1)
    vals = jnp.where(lane < N_E, logits, NEG)
    tops, idxs = [], []
    for _ in range(TOP_K):
        m = jnp.max(vals, axis=-1, keepdims=True)
        idx = jnp.min(jnp.where(vals == m, lane, 128), axis=-1, keepdims=True)
        tops.append(m)
        idxs.append(idx)
        vals = jnp.where(lane == idx, NEG, vals)
    es = [jnp.exp(t - tops[0]) for t in tops]
    den = es[0] + es[1] + es[2] + es[3]
    ti = jnp.zeros(logits.shape, I32)
    tg = jnp.zeros(logits.shape, F32)
    for k in range(TOP_K):
        ti = jnp.where(lane == k, idxs[k], ti)
        tg = jnp.where(lane == k, es[k] / den, tg)
    ti_ref[...] = ti
    tg_ref[...] = tg


def _router(h1, g_ffn, w_router, b_router):
    wr = jnp.pad(w_router, ((0, 0), (0, 128 - N_E)))
    br = jnp.pad(b_router, (0, 128 - N_E)).reshape(1, 128)
    return pl.pallas_call(
        _router_kernel,
        grid=(T_ALL // RT,),
        in_specs=[pl.BlockSpec((RT, D), lambda i: (i, 0)),
                  pl.BlockSpec((1, D), lambda i: (0, 0)),
                  pl.BlockSpec((D, 128), lambda i: (0, 0)),
                  pl.BlockSpec((1, 128), lambda i: (0, 0))],
        out_specs=[pl.BlockSpec((RT, D), lambda i: (i, 0)),
                   pl.BlockSpec((RT, 128), lambda i: (i, 0)),
                   pl.BlockSpec((RT, 128), lambda i: (i, 0))],
        out_shape=[jax.ShapeDtypeStruct((T_ALL, D), F32),
                   jax.ShapeDtypeStruct((T_ALL, 128), I32),
                   jax.ShapeDtypeStruct((T_ALL, 128), F32)],
        compiler_params=_cparams(1),
        name="router",
    )(h1, g_ffn.reshape(1, D), wr, br)


def _row_copy(src_hbm, dst, sem, src_row, dst_row):
    return pltpu.make_async_copy(src_hbm.at[pl.ds(src_row, 1)], dst.at[pl.ds(dst_row, 1)], sem)


def _gather_kernel(valid_ref, blk_ref, tok_ref, hn_hbm, o_ref, buf_ref, sem):
    s = pl.program_id(0)

    @pl.when(valid_ref[s // (TM_E // GT)] == 1)
    def _():
        def issue(r, carry):
            _row_copy(hn_hbm, buf_ref, sem, tok_ref[0, r], r).start()
            return carry

        def drain(r, carry):
            _row_copy(hn_hbm, buf_ref, sem, 0, r).wait()
            return carry

        lax.fori_loop(0, GT, issue, 0)
        lax.fori_loop(0, GT, drain, 0)
        o_ref[...] = buf_ref[...].astype(BF16)


def _moe_gather(hn, tok_of_slot, blk_valid, blk_row):
    per = TM_E // GT
    n_steps = NB_E * per

    def step_idx(s, valid, blk):
        k = s // per
        return jnp.where(valid[k] == 1, s, blk[k] * per + per - 1)

    grid_spec = pltpu.PrefetchScalarGridSpec(
        num_scalar_prefetch=2,
        grid=(n_steps,),
        in_specs=[pl.BlockSpec((None, 1, GT), lambda s, valid, blk: (step_idx(s, valid, blk), 0, 0),
                               memory_space=pltpu.SMEM),
                  pl.BlockSpec(memory_space=pl.ANY)],
        out_specs=pl.BlockSpec((GT, D), lambda s, valid, blk: (step_idx(s, valid, blk), 0)),
        scratch_shapes=[pltpu.VMEM((GT, D), F32), pltpu.SemaphoreType.DMA(())],
    )
    return pl.pallas_call(
        _gather_kernel,
        grid_spec=grid_spec,
        out_shape=jax.ShapeDtypeStruct((CAP_E, D), BF16),
        compiler_params=_cparams(1),
        name="moe_gather",
    )(blk_valid, blk_row, tok_of_slot.reshape(n_steps, 1, GT), hn)


GU_TN = 512


def _swiglu_pairs(gu):
    outs = []
    for q in range(2):
        a = gu[:, 256 * q:256 * q + 128]
        b = gu[:, 256 * q + 128:256 * q + 256]
        lane = lax.broadcasted_iota(I32, a.shape, 1)
        even = (lane & 1) == 0
        glu = jnp.where(even, a, pltpu.roll(b, 1, 1))
        lin = jnp.where(even, pltpu.roll(a, 127, 1), b)
        glu = jnp.minimum(glu, LIMIT)
        lin = jnp.clip(lin, -LIMIT, LIMIT)
        outs.append(glu * jax.nn.sigmoid(ALPHA * glu) * (lin + 1.0))
    return jnp.concatenate(outs, axis=1)


def _gu_kernel(e_ref, valid_ref, row_ref, x_ref, w_ref, b_ref, o_ref):
    k = pl.program_id(0)

    @pl.when(valid_ref[k] == 1)
    def _():
        gu = jnp.dot(x_ref[...], w_ref[...].astype(BF16), preferred_element_type=F32) + b_ref[...]
        o_ref[...] = _swiglu_pairs(gu).astype(BF16)


DN_TN = 512


def _down_kernel(e_ref, valid_ref, row_ref, a_ref, w_ref, b_ref, o_ref, ws_ref):
    k = pl.program_id(0)

    @pl.when(valid_ref[k] == 1)
    def _():
        nc = DN_TN // 128
        for c in range(nc):
            w4 = w_ref[:, 128 * c:128 * c + 128].reshape(D // 128, 2, HD, 128)
            for b in range(2):
                ws_ref[c, pl.ds(b, D // 2, stride=2), :] = w4[:, b].reshape(D // 2, 128)
        wb = jnp.concatenate([ws_ref[c] for c in range(nc)], axis=1).astype(BF16)
        o_ref[...] = jnp.dot(a_ref[...], wb, preferred_element_type=F32) + b_ref[...]


def _grouped(kernel, lhs, w, b, blk_e, blk_valid, blk_row, tn, tn_out, out_dtype, scratch, name):
    n_w = w.shape[2]
    nj = n_w // tn

    def jj(k, j, valid):
        return jnp.where(valid[k] == 1, j, nj - 1)

    grid_spec = pltpu.PrefetchScalarGridSpec(
        num_scalar_prefetch=3,
        grid=(NB_E, nj),
        in_specs=[pl.BlockSpec((TM_E, D), lambda k, j, e, valid, row: (row[k], 0),
                               pipeline_mode=pl.Buffered(1)),
                  pl.BlockSpec((None, D, tn), lambda k, j, e, valid, row: (e[k], 0, jj(k, j, valid))),
                  pl.BlockSpec((None, 1, tn), lambda k, j, e, valid, row: (e[k], 0, jj(k, j, valid)))],
        out_specs=pl.BlockSpec((TM_E, tn_out), lambda k, j, e, valid, row: (row[k], jj(k, j, valid))),
        scratch_shapes=scratch,
    )
    return pl.pallas_call(
        kernel,
        grid_spec=grid_spec,
        out_shape=jax.ShapeDtypeStruct((CAP_E, nj * tn_out), out_dtype),
        compiler_params=_cparams(2),
        name=name,
    )(blk_e, blk_valid, blk_row, lhs, w, b.reshape(N_E, 1, n_w))


def _combine_kernel(slot_ref, g_ref, h_ref, gp_ref, y_hbm, h2_ref, hn_ref, buf_ref, sem):
    def issue(r, carry):
        for k in range(TOP_K):
            _row_copy(y_hbm, buf_ref.at[k], sem, slot_ref[0, TOP_K * r + k], r).start()
        return carry

    def drain(r, carry):
        for k in range(TOP_K):
            _row_copy(y_hbm, buf_ref.at[k], sem, 0, r).wait()
        return carry

    lax.fori_loop(0, GT, issue, 0)
    lax.fori_loop(0, GT, drain, 0)
    g = g_ref[...]
    h2 = h_ref[...]
    for k in range(TOP_K):
        h2 = h2 + g[:, k:k + 1] * buf_ref[k]
    h2_ref[...] = h2
    ms = jnp.mean(h2 * h2, axis=-1, keepdims=True)
    hn_ref[...] = (h2 * lax.rsqrt(ms + EPS) * gp_ref[...]).astype(BF16)


def _moe_combine(slot, gates, h1, g_ple, y_sorted):
    return pl.pallas_call(
        _combine_kernel,
        grid=(T_ALL // GT,),
        in_specs=[pl.BlockSpec((None, 1, GT * TOP_K), lambda i: (i, 0, 0), memory_space=pltpu.SMEM),
                  pl.BlockSpec((GT, 128), lambda i: (i, 0)),
                  pl.BlockSpec((GT, D), lambda i: (i, 0)),
                  pl.BlockSpec((1, D), lambda i: (0, 0)),
                  pl.BlockSpec(memory_space=pl.ANY)],
        out_specs=[pl.BlockSpec((GT, D), lambda i: (i, 0)),
                   pl.BlockSpec((GT, D), lambda i: (i, 0))],
        out_shape=[jax.ShapeDtypeStruct((T_ALL, D), F32),
                   jax.ShapeDtypeStruct((T_ALL, D), BF16)],
        scratch_shapes=[pltpu.VMEM((TOP_K, GT, D), F32), pltpu.SemaphoreType.DMA(())],
        compiler_params=_cparams(1),
        name="moe_combine",
    )(slot.reshape(T_ALL // GT, 1, GT * TOP_K), gates, h1, g_ple.reshape(1, D), y_sorted)


def _routing_tables(top_i):
    flat_e = top_i.reshape(-1)
    onehot = (flat_e[:, None] == jnp.arange(N_E, dtype=I32)[None, :]).astype(I32)
    csum = jnp.cumsum(onehot, axis=0)
    rank = jnp.take_along_axis(csum, flat_e[:, None], axis=1)[:, 0] - 1
    counts = csum[-1]
    nblk = (counts + TM_E - 1) // TM_E
    blk_end = jnp.cumsum(nblk)
    blk_start = blk_end - nblk
    slot = (blk_start[flat_e] * TM_E + rank).astype(I32)
    n_used = blk_end[-1]
    ks = jnp.arange(NB_E, dtype=I32)
    valid = ks < n_used
    k_eff = jnp.where(valid, ks, n_used - 1)
    blk_e = jnp.minimum(jnp.searchsorted(blk_end, k_eff, side="right"), N_E - 1).astype(I32)
    flat_t = jnp.repeat(jnp.arange(T_ALL, dtype=I32), TOP_K)
    tok_of_slot = jnp.zeros((CAP_E,), I32).at[slot].set(flat_t)
    return slot, tok_of_slot, blk_e, valid.astype(I32), k_eff.astype(I32)


def kernel(x_prompt, x_sample, cache_k, cache_v, state_conv, p_prompt, p_sample, g_mix, w_in, b_in, conv_w, sinks, w_out, b_out, g_ffn, w_router, b_router, w_gate_up, b_gate_up, w_down, b_down, g_ple, w_ple_gate, w_ple, g_final):
    x_all = jnp.concatenate([x_prompt[0], x_sample.reshape(T_S, D)], axis=0)
    p_all = jnp.concatenate([p_prompt[0, 0], p_sample[0].reshape(T_S, PLE)], axis=0)

    u = _rms_norm(x_all, g_mix[0], BF16)
    z = _dense(_mm_bias_kernel, u, w_in[0], [b_in[0].reshape(1, N_IN)],
               [pl.BlockSpec((1, 512), lambda i, j: (0, j))], 1024, 512, "in_proj")

    y_p, krot_p, y_s, krot_s = _attention(z, cache_k[0], cache_v[0], sinks[0])
    y_att = jnp.concatenate([y_p, y_s], axis=0)

    merged, cx = _merge(z, y_att, state_conv[0], conv_w[0])
    h1 = _dense(_mm_bias_res_kernel, merged, w_out[0], [b_out[0].reshape(1, D), x_all],
                [pl.BlockSpec((1, 512), lambda i, j: (0, j)),
                 pl.BlockSpec((1024, 512), lambda i, j: (i, j))], 1024, 512, "out_proj")

    hn, top_i, top_g = _router(h1, g_ffn[0], w_router[0], b_router[0])
    slot, tok_of_slot, blk_e, blk_valid, blk_row = _routing_tables(top_i[:, :TOP_K])
    x_sorted = _moe_gather(hn, tok_of_slot, blk_valid, blk_row)
    act = _grouped(_gu_kernel, x_sorted, w_gate_up[0], b_gate_up[0], blk_e, blk_valid, blk_row,
                   GU_TN, GU_TN // 2, BF16, [], "moe_gate_up")
    y_sorted = _grouped(_down_kernel, act, w_down[0], b_down[0], blk_e, blk_valid, blk_row,
                        DN_TN, DN_TN, F32, [pltpu.VMEM((DN_TN // 128, D, 128), F32)], "moe_down")
    h2, hn2 = _moe_combine(slot, top_g, h1, g_ple[0], y_sorted)

    h3 = _dense(_mm_ple_kernel, hn2, w_ple_gate[0], [h2, p_all, w_ple[0]],
                [pl.BlockSpec((1024, 512), lambda i, j: (i, j)),
                 pl.BlockSpec((1024, PLE), lambda i, j: (i, 0)),
                 pl.BlockSpec((PLE, 512), lambda i, j: (0, j))], 1024, 512, "ple")

    y_prompt = _rms_norm(h3, g_final, F32, 0, T_P).reshape(1, T_P, D)
    y_sample = _rms_norm(h3, g_final, F32, T_P, T_S).reshape(N_SEQ, S_LEN, D)

    w_p = min(WIN, T_P)
    k_win_p = krot_p[T_P - w_p:].reshape(1, 1, w_p, N_KV, HD)
    v_win_p = z[T_P - w_p:T_P, C_V * CB:C_V * CB + KV_W].reshape(1, 1, w_p, N_KV, HD)
    conv_p = cx[T_P - 2:T_P].reshape(1, 1, 2, D)
    k_new = krot_s.reshape(N_SEQ, S_LEN, N_KV, HD)
    v_new = z[T_P:, C_V * CB:C_V * CB + KV_W].reshape(N_SEQ, S_LEN, N_KV, HD)
    k_win_s = jnp.concatenate([cache_k[0][:, S_LEN:], k_new], axis=1)[None]
    v_win_s = jnp.concatenate([cache_v[0][:, S_LEN:], v_new], axis=1)[None]
    conv_s = cx[T_P:].reshape(N_SEQ, S_LEN, D)[:, S_LEN - 2:][None]
    return (y_prompt, y_sample, k_win_p, v_win_p, conv_p, k_win_s, v_win_s, conv_s)
```

```python
import functools

import jax
import jax.numpy as jnp
import numpy as np
from jax import lax
from jax.experimental import pallas as pl
from jax.experimental.pallas import tpu as pltpu

BF16 = jnp.bfloat16
F32 = jnp.float32
I32 = jnp.int32
U32 = jnp.uint32

D = 4096
T_P = 8192
N_SEQ = 128
S_LEN = 8
T_S = N_SEQ * S_LEN
T_ALL = T_P + T_S
PAST = 8192
HD = 64
N_KV = 8
KV_W = N_KV * HD
ROT = 16
THETA = 500000.0
WIN = 128
N_E = 32
TOP_K = 4
N_ASSIGN = T_ALL * TOP_K
LIMIT = 7.0
ALPHA = 1.702
PLE = 256
EPS = 1e-5
NEG = -1e30
N_IN = 25600

CB = 512
C_K, C_V, C_XC, C_B, C_C, C_GA, C_GC = 8, 9, 10, 18, 26, 34, 42

VMEM_LIMIT = 56 * 1024 * 1024

TM_E = 1280
NB_E = N_ASSIGN // TM_E + N_E + 1
CAP_E = NB_E * TM_E
GT = 256
SEQ_PER_STEP = 8


def _cparams(n_axes):
    return pltpu.CompilerParams(dimension_semantics=("arbitrary",) * n_axes,
                                vmem_limit_bytes=VMEM_LIMIT)


def _rms_cast_kernel(x_ref, g_ref, o_ref):
    x = x_ref[...]
    ms = jnp.mean(x * x, axis=-1, keepdims=True)
    o_ref[...] = (x * lax.rsqrt(ms + EPS) * g_ref[...]).astype(o_ref.dtype)


def _rms_norm(x, g, out_dtype, row0=0, rows=None, tm=256):
    rows = x.shape[0] if rows is None else rows
    b0 = row0 // tm
    return pl.pallas_call(
        _rms_cast_kernel,
        grid=(rows // tm,),
        in_specs=[pl.BlockSpec((tm, D), lambda i: (i + b0, 0)),
                  pl.BlockSpec((1, D), lambda i: (0, 0))],
        out_specs=pl.BlockSpec((tm, D), lambda i: (i, 0)),
        out_shape=jax.ShapeDtypeStruct((rows, D), out_dtype),
        compiler_params=_cparams(1),
        name="rms_norm",
    )(x, g.reshape(1, D))


def _mm_bias_kernel(lhs_ref, w_ref, b_ref, o_ref):
    acc = jnp.dot(lhs_ref[...], w_ref[...].astype(BF16), preferred_element_type=F32)
    o_ref[...] = acc + b_ref[...]


def _mm_bias_res_kernel(lhs_ref, w_ref, b_ref, r_ref, o_ref):
    acc = jnp.dot(lhs_ref[...], w_ref[...].astype(BF16), preferred_element_type=F32)
    o_ref[...] = acc + b_ref[...] + r_ref[...]


def _mm_ple_kernel(lhs_ref, w_ref, h_ref, p_ref, wp_ref, o_ref):
    acc = jnp.dot(lhs_ref[...], w_ref[...].astype(BF16), preferred_element_type=F32)
    pp = jnp.dot(p_ref[...].astype(BF16), wp_ref[...].astype(BF16), preferred_element_type=F32)
    o_ref[...] = h_ref[...] + jax.nn.sigmoid(acc) * pp


def _dense(kernel, lhs, w, extras, extra_specs, tm, tn, name):
    M, K = lhs.shape
    N = w.shape[1]
    return pl.pallas_call(
        kernel,
        grid=(M // tm, N // tn),
        in_specs=[pl.BlockSpec((tm, K), lambda i, j: (i, 0)),
                  pl.BlockSpec((K, tn), lambda i, j: (0, j))] + extra_specs,
        out_specs=pl.BlockSpec((tm, tn), lambda i, j: (i, j)),
        out_shape=jax.ShapeDtypeStruct((M, N), F32),
        compiler_params=_cparams(2),
        name=name,
    )(lhs, w, *extras)


def _rope(x, c, s1, s2):
    w = x.shape[1]
    reps = w // 128
    ct = jnp.concatenate([c] * reps, axis=1) if reps > 1 else c
    s1t = jnp.concatenate([s1] * reps, axis=1) if reps > 1 else s1
    s2t = jnp.concatenate([s2] * reps, axis=1) if reps > 1 else s2
    return x * ct + pltpu.roll(x, 8, 1) * s1t + pltpu.roll(x, w - 8, 1) * s2t


def _half_placed(chunk, half):
    lane = lax.broadcasted_iota(I32, chunk.shape, 1)
    low = lane < HD
    other = pltpu.roll(chunk, HD, 1)
    zero = jnp.zeros_like(chunk)
    first = half == 0
    lo = jnp.where(low, jnp.where(first, chunk, other), zero)
    hi = jnp.where(low, zero, jnp.where(first, other, chunk))
    return lo.astype(BF16), hi.astype(BF16)


def _attend_group(g, q_rot, keys, vals, mask, sink_ref, o_ref):
    k_lo, k_hi = _half_placed(keys, g % 2)
    v_lo, v_hi = _half_placed(vals, g % 2)
    for i in range(4):
        col = 128 * i
        qp = (q_rot[:, col:col + 128] * (HD ** -0.5)).astype(BF16)
        out = None
        for par, (k2, v2) in enumerate(((k_lo, v_lo), (k_hi, v_hi))):
            sink = sink_ref[8 * g + 2 * i + par]
            s = lax.dot_general(qp, k2, (((1,), (1,)), ((), ())), preferred_element_type=F32)
            s = jnp.where(mask, s, NEG)
            m = jnp.maximum(jnp.max(s, axis=-1, keepdims=True), sink)
            p = jnp.exp(s - m)
            den = jnp.sum(p, axis=-1, keepdims=True) + jnp.exp(sink - m)
            o = jnp.dot(p.astype(BF16), v2, preferred_element_type=F32) / den
            out = o if out is None else out + o
        o_ref[:, col:col + 128] = out.astype(o_ref.dtype)


def _attn_prompt_kernel(sink_ref, q_ref, kc_ref, kp_ref, vc_ref, vp_ref,
                        cc_ref, s1c_ref, s2c_ref, cp_ref, s1p_ref, s2p_ref,
                        o_ref, krot_ref):
    b = pl.program_id(0)
    g = pl.program_id(1)
    q_rot = _rope(q_ref[...], cc_ref[...], s1c_ref[...], s2c_ref[...])
    k_cur = _rope(kc_ref[...], cc_ref[...], s1c_ref[...], s2c_ref[...])
    k_prev = _rope(kp_ref[...], cp_ref[...], s1p_ref[...], s2p_ref[...])
    krot_ref[...] = k_cur
    keys = jnp.concatenate([k_prev, k_cur], axis=0)
    vals = jnp.concatenate([vp_ref[...], vc_ref[...]], axis=0)
    r = lax.broadcasted_iota(I32, (128, 256), 0)
    c = lax.broadcasted_iota(I32, (128, 256), 1)
    mask = (c > r) & (c <= r + WIN) & ((b > 0) | (c >= 128))
    _attend_group(g, q_rot, keys, vals, mask, sink_ref, o_ref)


def _attn_sample_kernel(sink_ref, q_ref, kn_ref, vn_ref, ck_ref, cv_ref,
                        c_ref, s1_ref, s2_ref, mask_ref, o_ref, krot_ref):
    g = pl.program_id(1)
    q_rot = _rope(q_ref[...], c_ref[...], s1_ref[...], s2_ref[...])
    k_new = _rope(kn_ref[...], c_ref[...], s1_ref[...], s2_ref[...])
    krot_ref[...] = k_new
    keys = jnp.concatenate([ck_ref[...], k_new], axis=0)
    vals = jnp.concatenate([cv_ref[...], vn_ref[...]], axis=0)
    mask = mask_ref[...] > 0.5
    _attend_group(g, q_rot, keys, vals, mask, sink_ref, o_ref)


def _rope_tables(pos):
    half = ROT // 2
    inv_freq = THETA ** (-jnp.arange(half, dtype=F32) / half)
    ang = pos.astype(F32)[:, None] * inv_freq[None, :]
    cos, sin = jnp.cos(ang), jnp.sin(ang)
    n = pos.shape[0]
    ones = jnp.ones((n, HD - ROT), F32)
    zeros = jnp.zeros((n, HD - ROT), F32)
    z8 = jnp.zeros((n, half), F32)
    c = jnp.concatenate([cos, cos, ones], axis=1)
    s1 = jnp.concatenate([z8, sin, zeros], axis=1)
    s2 = jnp.concatenate([-sin, z8, zeros], axis=1)
    return tuple(jnp.concatenate([t, t], axis=1) for t in (c, s1, s2))


def _sample_mask():
    sb = SEQ_PER_STEP
    r = np.arange(sb * S_LEN)
    sq, qi = r // S_LEN, r % S_LEN
    c = np.arange(sb * WIN)
    m_cache = (c[None, :] // WIN == sq[:, None]) & (c[None, :] % WIN > qi[:, None])
    cn = np.arange(sb * S_LEN)
    m_new = (cn[None, :] // S_LEN == sq[:, None]) & (cn[None, :] % S_LEN <= qi[:, None])
    return np.concatenate([m_cache, m_new], axis=1).astype(np.float32)


def _attention(z, cache_k, cache_v, sinks):
    smem = pl.BlockSpec(memory_space=pltpu.SMEM)
    tabs_p = _rope_tables(jnp.arange(T_P))
    prev = lambda b: jnp.maximum(b - 1, 0)
    ck0, cv0 = C_K * CB // 128, C_V * CB // 128
    tab_cur = pl.BlockSpec((128, 128), lambda b, g: (b, 0))
    tab_prev = pl.BlockSpec((128, 128), lambda b, g: (prev(b), 0))
    y_p, krot_p = pl.pallas_call(
        _attn_prompt_kernel,
        grid=(T_P // 128, N_KV),
        in_specs=[smem,
                  pl.BlockSpec((128, CB), lambda b, g: (b, g)),
                  pl.BlockSpec((128, 128), lambda b, g: (b, ck0 + g // 2)),
                  pl.BlockSpec((128, 128), lambda b, g: (prev(b), ck0 + g // 2)),
                  pl.BlockSpec((128, 128), lambda b, g: (b, cv0 + g // 2)),
                  pl.BlockSpec((128, 128), lambda b, g: (prev(b), cv0 + g // 2)),
                  tab_cur, tab_cur, tab_cur, tab_prev, tab_prev, tab_prev],
        out_specs=[pl.BlockSpec((128, CB), lambda b, g: (b, g)),
                   pl.BlockSpec((128, 128), lambda b, g: (b, g // 2))],
        out_shape=[jax.ShapeDtypeStruct((T_P, D), BF16),
                   jax.ShapeDtypeStruct((T_P, KV_W), F32)],
        compiler_params=_cparams(2),
        name="attn_prompt",
    )(sinks, z, z, z, z, z, *tabs_p, *tabs_p)

    sb = SEQ_PER_STEP
    rows = sb * S_LEN
    tabs_s = _rope_tables(PAST + (jnp.arange(rows) % S_LEN))
    mask = jnp.asarray(_sample_mask())
    nk = sb * WIN + rows
    rb0 = T_P // rows
    const = lambda shape: pl.BlockSpec(shape, lambda b, g: (0, 0))
    y_s, krot_s = pl.pallas_call(
        _attn_sample_kernel,
        grid=(N_SEQ // sb, N_KV),
        in_specs=[smem,
                  pl.BlockSpec((rows, CB), lambda b, g: (b + rb0, g)),
                  pl.BlockSpec((rows, 128), lambda b, g: (b + rb0, ck0 + g // 2)),
                  pl.BlockSpec((rows, 128), lambda b, g: (b + rb0, cv0 + g // 2)),
                  pl.BlockSpec((sb * WIN, 128), lambda b, g: (b, g // 2)),
                  pl.BlockSpec((sb * WIN, 128), lambda b, g: (b, g // 2)),
                  const((rows, 128)), const((rows, 128)), const((rows, 128)),
                  const((rows, nk))],
        out_specs=[pl.BlockSpec((rows, CB), lambda b, g: (b, g)),
                   pl.BlockSpec((rows, 128), lambda b, g: (b, g // 2))],
        out_shape=[jax.ShapeDtypeStruct((T_S, D), BF16),
                   jax.ShapeDtypeStruct((T_S, KV_W), F32)],
        compiler_params=_cparams(2),
        name="attn_sample",
    )(sinks, z, z, z, cache_k.reshape(N_SEQ * WIN, KV_W), cache_v.reshape(N_SEQ * WIN, KV_W),
      *tabs_s, mask)
    return y_p, krot_p, y_s, krot_s


MT = 512


def _merge_kernel(ya_ref, xc_ref, bg_ref, cg_ref, ga_ref, gc_ref, xc8_ref, cg8_ref,
                  s1_ref, s2_ref, cw_ref, m_ref, cx_ref):
    i = pl.program_id(1)
    n_p = T_P // MT
    is_sample = i >= n_p
    cx = cg_ref[...] * xc_ref[...]
    cx_ref[...] = cx
    row = lax.broadcasted_iota(I32, cx.shape, 0)
    r1 = pltpu.roll(cx, 1, 0)
    r2 = pltpu.roll(cx, 2, 0)
    p8 = cg8_ref[...] * xc8_ref[...] * jnp.where(i > 0, 1.0, 0.0)
    pad = jnp.zeros((MT - 8, cx.shape[1]), F32)
    p1 = jnp.concatenate([pltpu.roll(p8, 1, 0), pad], axis=0)
    p2 = jnp.concatenate([pltpu.roll(p8, 2, 0), pad], axis=0)
    c1_p = jnp.where(row >= 1, r1, p1)
    c2_p = jnp.where(row >= 2, r2, p2)
    c1_s = jnp.where(row % S_LEN >= 1, r1, s1_ref[...])
    c2_s = jnp.where(row % S_LEN >= 2, r2, s2_ref[...])
    c1 = jnp.where(is_sample, c1_s, c1_p)
    c2 = jnp.where(is_sample, c2_s, c2_p)
    w = cw_ref[...]
    y_conv = bg_ref[...] * (w[0:1] * c2 + w[1:2] * c1 + w[2:3] * cx)
    merged = jax.nn.sigmoid(ga_ref[...]) * ya_ref[...].astype(F32) + jax.nn.sigmoid(gc_ref[...]) * y_conv
    m_ref[...] = merged.astype(m_ref.dtype)


def _merge(z, y_att, state_conv, conv_w):
    n_p = T_P // MT
    zb = lambda off: pl.BlockSpec((MT, MT), lambda j, i: (i, off + j))
    z8 = lambda off: pl.BlockSpec((8, MT), lambda j, i: (jnp.maximum(i * (MT // 8) - 1, 0), off + j))
    sblk = pl.BlockSpec((MT, MT), lambda j, i: (jnp.maximum(i - n_p, 0), j))
    st = state_conv
    zrow = jnp.zeros((N_SEQ, 1, D), F32)
    s1 = jnp.concatenate([st[:, 1:2], jnp.zeros((N_SEQ, S_LEN - 1, D), F32)], axis=1).reshape(T_S, D)
    s2 = jnp.concatenate([st[:, 0:1], st[:, 1:2], jnp.zeros((N_SEQ, S_LEN - 2, D), F32)], axis=1).reshape(T_S, D)
    del zrow
    cw = jnp.concatenate([conv_w, jnp.zeros((5, D), F32)], axis=0)
    return pl.pallas_call(
        _merge_kernel,
        grid=(D // MT, T_ALL // MT),
        in_specs=[pl.BlockSpec((MT, MT), lambda j, i: (i, j)),
                  zb(C_XC), zb(C_B), zb(C_C), zb(C_GA), zb(C_GC), z8(C_XC), z8(C_C),
                  sblk, sblk,
                  pl.BlockSpec((8, MT), lambda j, i: (0, j))],
        out_specs=[pl.BlockSpec((MT, MT), lambda j, i: (i, j)),
                   pl.BlockSpec((MT, MT), lambda j, i: (i, j))],
        out_shape=[jax.ShapeDtypeStruct((T_ALL, D), BF16),
                   jax.ShapeDtypeStruct((T_ALL, D), F32)],
        compiler_params=_cparams(2),
        name="merge",
    )(y_att, z, z, z, z, z, z, z, s1, s2, cw)


RT = 256


def _pack_bf16_pair(lo, hi):
    lo_bits = pltpu.bitcast(lo.astype(BF16).astype(F32), U32)
    hi_bits = pltpu.bitcast(hi.astype(BF16).astype(F32), U32)
    return (lo_bits >> 16) | (hi_bits & jnp.uint32(0xFFFF0000))


def _unpack_bf16_pair(w):
    lo = pltpu.bitcast(w << 16, F32)
    hi = pltpu.bitcast(w & jnp.uint32(0xFFFF0000), F32)
    return lo, hi


def _router_kernel(h_ref, g_ref, wr_ref, br_ref, tri_ref, hn_ref, ti_ref, tg_ref, rk_ref, cnt_ref,
                   base_ref):
    @pl.when(pl.program_id(0) == 0)
    def _():
        base_ref[...] = jnp.zeros_like(base_ref)

    x = h_ref[...]
    ms = jnp.mean(x * x, axis=-1, keepdims=True)
    hn = x * lax.rsqrt(ms + EPS) * g_ref[...]
    hn_ref[...] = _pack_bf16_pair(hn[:, :D // 2], hn[:, D // 2:])
    w = wr_ref[...]
    h_hi = hn.astype(BF16)
    h_lo = (hn - h_hi.astype(F32)).astype(BF16)
    w_hi = w.astype(BF16)
    w_lo = (w - w_hi.astype(F32)).astype(BF16)
    logits = (jnp.dot(h_hi, w_hi, preferred_element_type=F32)
              + jnp.dot(h_hi, w_lo, preferred_element_type=F32)
              + jnp.dot(h_lo, w_hi, preferred_element_type=F32)) + br_ref[...]
    lane = lax.broadcasted_iota(I32, logits.shape, 1)
    vals = jnp.where(lane < N_E, logits, NEG)
    tops, idxs = [], []
    for _ in range(TOP_K):
        m = jnp.max(vals, axis=-1, keepdims=True)
        idx = jnp.min(jnp.where(vals == m, lane, 128), axis=-1, keepdims=True)
        tops.append(m)
        idxs.append(idx)
        vals = jnp.where(lane == idx, NEG, vals)
    es = [jnp.exp(t - tops[0]) for t in tops]
    den = es[0] + es[1] + es[2] + es[3]
    onehot = jnp.zeros(logits.shape, F32)
    for k in range(TOP_K):
        onehot = onehot + jnp.where(lane == idxs[k], 1.0, 0.0)
    before = jnp.dot(tri_ref[...], onehot.astype(BF16), preferred_element_type=F32) + base_ref[0:1, :]
    ti = jnp.zeros(logits.shape, I32)
    tg = jnp.zeros(logits.shape, F32)
    rk = jnp.zeros(logits.shape, F32)
    for k in range(TOP_K):
        ti = jnp.where(lane == k, idxs[k], ti)
        tg = jnp.where(lane == k, es[k] / den, tg)
        r = jnp.sum(jnp.where(lane == idxs[k], before, 0.0), axis=-1, keepdims=True)
        rk = jnp.where(lane == k, r, rk)
    ti_ref[...] = ti
    tg_ref[...] = tg
    rk_ref[...] = rk.astype(I32)
    total = base_ref[...] + jnp.sum(onehot, axis=0, keepdims=True)
    base_ref[...] = total
    cnt_ref[...] = total.astype(I32)


def _router(h1, g_ffn, w_router, b_router):
    wr = jnp.pad(w_router, ((0, 0), (0, 128 - N_E)))
    br = jnp.pad(b_router, (0, 128 - N_E)).reshape(1, 128)
    tri = jnp.asarray(np.tril(np.ones((RT, RT), np.float32), -1), BF16)
    blk128 = pl.BlockSpec((RT, 128), lambda i: (i, 0))
    return pl.pallas_call(
        _router_kernel,
        grid=(T_ALL // RT,),
        in_specs=[pl.BlockSpec((RT, D), lambda i: (i, 0)),
                  pl.BlockSpec((1, D), lambda i: (0, 0)),
                  pl.BlockSpec((D, 128), lambda i: (0, 0)),
                  pl.BlockSpec((1, 128), lambda i: (0, 0)),
                  pl.BlockSpec((RT, RT), lambda i: (0, 0))],
        out_specs=[pl.BlockSpec((RT, D // 2), lambda i: (i, 0)),
                   blk128, blk128, blk128,
                   pl.BlockSpec((8, 128), lambda i: (0, 0))],
        out_shape=[jax.ShapeDtypeStruct((T_ALL, D // 2), U32),
                   jax.ShapeDtypeStruct((T_ALL, 128), I32),
                   jax.ShapeDtypeStruct((T_ALL, 128), F32),
                   jax.ShapeDtypeStruct((T_ALL, 128), I32),
                   jax.ShapeDtypeStruct((8, 128), I32)],
        scratch_shapes=[pltpu.VMEM((8, 128), F32)],
        compiler_params=_cparams(1),
        name="router",
    )(h1, g_ffn.reshape(1, D), wr, br, tri)


def _row_copy(src_hbm, dst, sem, src_row, dst_row):
    return pltpu.make_async_copy(src_hbm.at[pl.ds(src_row, 1)], dst.at[pl.ds(dst_row, 1)], sem)


GU_TN = 512
GU_NJ = 2 * D // GU_TN
GU_ROWS = TM_E // GU_NJ


def _swiglu_pairs(gu):
    outs = []
    for q in range(2):
        a = gu[:, 256 * q:256 * q + 128]
        b = gu[:, 256 * q + 128:256 * q + 256]
        lane = lax.broadcasted_iota(I32, a.shape, 1)
        even = (lane & 1) == 0
        glu = jnp.where(even, a, pltpu.roll(b, 1, 1))
        lin = jnp.where(even, pltpu.roll(a, 127, 1), b)
        glu = jnp.minimum(glu, LIMIT)
        lin = jnp.clip(lin, -LIMIT, LIMIT)
        outs.append(glu * jax.nn.sigmoid(ALPHA * glu) * (lin + 1.0))
    return jnp.concatenate(outs, axis=1)


def _pair_perm():
    p = np.zeros((GU_TN // 2, GU_TN // 2), np.float32)
    for q in range(GU_TN // 256):
        for j in range(HD):
            for b in range(2):
                p[128 * q + 2 * j + b, 128 * q + HD * b + j] = 1.0
    return p


def _gu_kernel(e_ref, valid_ref, row_ref,
               tok0_ref, tokn_ref, hn_hbm, w_ref, b_ref, p_ref, o_ref,
               stage_ref, lhs_ref, sem):
    k = pl.program_id(0)
    j = pl.program_id(1)
    valid = valid_ref[k] == 1

    def fetch(tok_ref, r0, n):
        for r in range(n):
            _row_copy(hn_hbm, stage_ref, sem, tok_ref[0, r0 + r], r0 + r).start()

    @pl.when(j == 0)
    def _():
        @pl.when(k == 0)
        def _():
            def body(c, carry):
                fetch(tok0_ref, c * GU_ROWS, GU_ROWS)
                return carry
            lax.fori_loop(0, GU_NJ, body, 0)

        prev_valid = (k > 0) & (valid_ref[jnp.maximum(k - 1, 0)] == 1)

        @pl.when(valid | prev_valid)
        def _():
            pltpu.make_async_copy(hn_hbm.at[pl.ds(0, TM_E)], stage_ref, sem).wait()

        @pl.when(valid)
        def _():
            def body(c, carry):
                rows = pl.ds(pl.multiple_of(c * 128, 128), 128)
                lo, hi = _unpack_bf16_pair(stage_ref[rows, :])
                lhs_ref[rows, 0:D // 2] = lo.astype(BF16)
                lhs_ref[rows, D // 2:D] = hi.astype(BF16)
                return carry
            lax.fori_loop(0, TM_E // 128, body, 0)

    @pl.when(valid)
    def _():
        fetch(tokn_ref, j * GU_ROWS, GU_ROWS)
        gu = jnp.dot(lhs_ref[...], w_ref[...].astype(BF16), preferred_element_type=F32) + b_ref[...]
        act = _swiglu_pairs(gu).astype(BF16)
        o_ref[...] = jnp.dot(act, p_ref[...], preferred_element_type=F32).astype(BF16)


def _moe_gate_up(hn_packed, tok_of_slot, w, b, blk_e, blk_valid, blk_row):
    nj = GU_NJ
    tn_out = GU_TN // 2

    def jj(k, j, valid):
        return jnp.where(valid[k] == 1, j, nj - 1)

    tok3 = tok_of_slot.reshape(NB_E, 1, TM_E)
    grid_spec = pltpu.PrefetchScalarGridSpec(
        num_scalar_prefetch=3,
        grid=(NB_E, nj),
        in_specs=[pl.BlockSpec((None, 1, TM_E), lambda k, j, e, valid, row: (0, 0, 0),
                               memory_space=pltpu.SMEM),
                  pl.BlockSpec((None, 1, TM_E),
                               lambda k, j, e, valid, row: (jnp.minimum(k + 1, NB_E - 1), 0, 0),
                               memory_space=pltpu.SMEM),
                  pl.BlockSpec(memory_space=pl.ANY),
                  pl.BlockSpec((None, D, GU_TN), lambda k, j, e, valid, row: (e[k], 0, jj(k, j, valid))),
                  pl.BlockSpec((None, 1, GU_TN), lambda k, j, e, valid, row: (e[k], 0, jj(k, j, valid))),
                  pl.BlockSpec((tn_out, tn_out), lambda k, j, e, valid, row: (0, 0))],
        out_specs=pl.BlockSpec((TM_E, tn_out), lambda k, j, e, valid, row: (row[k], jj(k, j, valid))),
        scratch_shapes=[pltpu.VMEM((TM_E, D // 2), U32), pltpu.VMEM((TM_E, D), BF16),
                        pltpu.SemaphoreType.DMA(())],
    )
    return pl.pallas_call(
        _gu_kernel,
        grid_spec=grid_spec,
        out_shape=jax.ShapeDtypeStruct((CAP_E, nj * tn_out), BF16),
        compiler_params=_cparams(2),
        name="moe_gate_up",
    )(blk_e, blk_valid, blk_row, tok3, tok3, hn_packed, w, b.reshape(N_E, 1, 2 * D),
      jnp.asarray(_pair_perm(), BF16))


DN_TN = 512


def _down_kernel(e_ref, valid_ref, row_ref, a_ref, w_ref, b_ref, o_ref):
    k = pl.program_id(0)

    @pl.when(valid_ref[k] == 1)
    def _():
        y = jnp.dot(a_ref[...], w_ref[...].astype(BF16), preferred_element_type=F32) + b_ref[...]
        o_ref[...] = _pack_bf16_pair(y[:, :DN_TN // 2], y[:, DN_TN // 2:])


def _moe_down(act, w, b, blk_e, blk_valid, blk_row):
    nj = D // DN_TN

    def jj(k, j, valid):
        return jnp.where(valid[k] == 1, j, nj - 1)

    grid_spec = pltpu.PrefetchScalarGridSpec(
        num_scalar_prefetch=3,
        grid=(NB_E, nj),
        in_specs=[pl.BlockSpec((TM_E, D), lambda k, j, e, valid, row: (row[k], 0)),
                  pl.BlockSpec((None, D, DN_TN), lambda k, j, e, valid, row: (e[k], 0, jj(k, j, valid))),
                  pl.BlockSpec((None, 1, DN_TN), lambda k, j, e, valid, row: (e[k], 0, jj(k, j, valid)))],
        out_specs=pl.BlockSpec((TM_E, DN_TN // 2), lambda k, j, e, valid, row: (row[k], jj(k, j, valid))),
    )
    return pl.pallas_call(
        _down_kernel,
        grid_spec=grid_spec,
        out_shape=jax.ShapeDtypeStruct((CAP_E, D // 2), U32),
        compiler_params=_cparams(2),
        name="moe_down",
    )(blk_e, blk_valid, blk_row, act, w, b.reshape(N_E, 1, D))


def _combine_kernel(slot_ref, slotn_ref, g_ref, h_ref, gp_ref, y_hbm, h2_ref, hn_ref, buf_ref, sems):
    i = pl.program_id(0)
    n = pl.num_programs(0)
    cur = i % 2

    def fetch(idx_ref, slot):
        def body(c, carry):
            for r in range(8):
                row = c * 8 + r
                for k in range(TOP_K):
                    _row_copy(y_hbm, buf_ref.at[slot, k], sems.at[slot],
                              idx_ref[0, TOP_K * row + k], row).start()
            return carry
        lax.fori_loop(0, GT // 8, body, 0)

    @pl.when(i == 0)
    def _():
        fetch(slot_ref, 0)

    @pl.when(i + 1 < n)
    def _():
        fetch(slotn_ref, 1 - cur)

    for k in range(TOP_K):
        pltpu.make_async_copy(y_hbm.at[pl.ds(0, GT)], buf_ref.at[cur, k], sems.at[cur]).wait()

    g = g_ref[...]
    tw = DN_TN // 2
    for c in range(D // DN_TN):
        lo_acc = h_ref[:, DN_TN * c:DN_TN * c + tw]
        hi_acc = h_ref[:, DN_TN * c + tw:DN_TN * (c + 1)]
        for k in range(TOP_K):
            lo, hi = _unpack_bf16_pair(buf_ref[cur, k, :, tw * c:tw * (c + 1)])
            lo_acc = lo_acc + g[:, k:k + 1] * lo
            hi_acc = hi_acc + g[:, k:k + 1] * hi
        h2_ref[:, DN_TN * c:DN_TN * c + tw] = lo_acc
        h2_ref[:, DN_TN * c + tw:DN_TN * (c + 1)] = hi_acc
    h2 = h2_ref[...]
    ms = jnp.mean(h2 * h2, axis=-1, keepdims=True)
    hn_ref[...] = (h2 * lax.rsqrt(ms + EPS) * gp_ref[...]).astype(BF16)


def _moe_combine(slot, gates, h1, g_ple, y_packed):
    n = T_ALL // GT
    slot3 = slot.reshape(n, 1, GT * TOP_K)
    return pl.pallas_call(
        _combine_kernel,
        grid=(n,),
        in_specs=[pl.BlockSpec((None, 1, GT * TOP_K), lambda i: (i, 0, 0), memory_space=pltpu.SMEM),
                  pl.BlockSpec((None, 1, GT * TOP_K), lambda i: (jnp.minimum(i + 1, n - 1), 0, 0),
                               memory_space=pltpu.SMEM),
                  pl.BlockSpec((GT, 128), lambda i: (i, 0)),
                  pl.BlockSpec((GT, D), lambda i: (i, 0)),
                  pl.BlockSpec((1, D), lambda i: (0, 0)),
                  pl.BlockSpec(memory_space=pl.ANY)],
        out_specs=[pl.BlockSpec((GT, D), lambda i: (i, 0)),
                   pl.BlockSpec((GT, D), lambda i: (i, 0))],
        out_shape=[jax.ShapeDtypeStruct((T_ALL, D), F32),
                   jax.ShapeDtypeStruct((T_ALL, D), BF16)],
        scratch_shapes=[pltpu.VMEM((2, TOP_K, GT, D // 2), U32), pltpu.SemaphoreType.DMA((2,))],
        compiler_params=_cparams(1),
        name="moe_combine",
    )(slot3, slot3, gates, h1, g_ple.reshape(1, D), y_packed)


def _routing_tables(top_i, rank, counts):
    flat_e = top_i.reshape(-1)
    nblk = (counts + TM_E - 1) // TM_E
    blk_end = jnp.cumsum(nblk)
    blk_start = blk_end - nblk
    slot = (blk_start[flat_e] * TM_E + rank.reshape(-1)).astype(I32)
    n_used = blk_end[-1]
    ks = jnp.arange(NB_E, dtype=I32)
    valid = ks < n_used
    k_eff = jnp.where(valid, ks, n_used - 1)
    blk_e = jnp.minimum(jnp.searchsorted(blk_end, k_eff, side="right"), N_E - 1).astype(I32)
    flat_t = jnp.repeat(jnp.arange(T_ALL, dtype=I32), TOP_K)
    tok_of_slot = jnp.zeros((CAP_E,), I32).at[slot].set(flat_t)
    return slot, tok_of_slot, blk_e, valid.astype(I32), k_eff.astype(I32)


def kernel(x_prompt, x_sample, cache_k, cache_v, state_conv, p_prompt, p_sample, g_mix, w_in, b_in, conv_w, sinks, w_out, b_out, g_ffn, w_router, b_router, w_gate_up, b_gate_up, w_down, b_down, g_ple, w_ple_gate, w_ple, g_final):
    x_all = jnp.concatenate([x_prompt[0], x_sample.reshape(T_S, D)], axis=0)
    p_all = jnp.concatenate([p_prompt[0, 0], p_sample[0].reshape(T_S, PLE)], axis=0)

    u = _rms_norm(x_all, g_mix[0], BF16)
    z = _dense(_mm_bias_kernel, u, w_in[0], [b_in[0].reshape(1, N_IN)],
               [pl.BlockSpec((1, 512), lambda i, j: (0, j))], 1024, 512, "in_proj")

    y_p, krot_p, y_s, krot_s = _attention(z, cache_k[0], cache_v[0], sinks[0])
    y_att = jnp.concatenate([y_p, y_s], axis=0)

    merged, cx = _merge(z, y_att, state_conv[0], conv_w[0])
    h1 = _dense(_mm_bias_res_kernel, merged, w_out[0], [b_out[0].reshape(1, D), x_all],
                [pl.BlockSpec((1, 512), lambda i, j: (0, j)),
                 pl.BlockSpec((1024, 512), lambda i, j: (i, j))], 1024, 512, "out_proj")

    hn_packed, top_i, top_g, rank, counts = _router(h1, g_ffn[0], w_router[0], b_router[0])
    slot, tok_of_slot, blk_e, blk_valid, blk_row = _routing_tables(
        top_i[:, :TOP_K], rank[:, :TOP_K], counts[0, :N_E])
    act = _moe_gate_up(hn_packed, tok_of_slot, w_gate_up[0], b_gate_up[0], blk_e, blk_valid, blk_row)
    y_packed = _moe_down(act, w_down[0], b_down[0], blk_e, blk_valid, blk_row)
    h2, hn2 = _moe_combine(slot, top_g, h1, g_ple[0], y_packed)

    h3 = _dense(_mm_ple_kernel, hn2, w_ple_gate[0], [h2, p_all, w_ple[0]],
                [pl.BlockSpec((1024, 512), lambda i, j: (i, j)),
                 pl.BlockSpec((1024, PLE), lambda i, j: (i, 0)),
                 pl.BlockSpec((PLE, 512), lambda i, j: (0, j))], 1024, 512, "ple")

    y_prompt = _rms_norm(h3, g_final, F32, 0, T_P).reshape(1, T_P, D)
    y_sample = _rms_norm(h3, g_final, F32, T_P, T_S).reshape(N_SEQ, S_LEN, D)

    w_p = min(WIN, T_P)
    k_win_p = krot_p[T_P - w_p:].reshape(1, 1, w_p, N_KV, HD)
    v_win_p = z[T_P - w_p:T_P, C_V * CB:C_V * CB + KV_W].reshape(1, 1, w_p, N_KV, HD)
    conv_p = cx[T_P - 2:T_P].reshape(1, 1, 2, D)
    k_new = krot_s.reshape(N_SEQ, S_LEN, N_KV, HD)
    v_new = z[T_P:, C_V * CB:C_V * CB + KV_W].reshape(N_SEQ, S_LEN, N_KV, HD)
    k_win_s = jnp.concatenate([cache_k[0][:, S_LEN:], k_new], axis=1)[None]
    v_win_s = jnp.concatenate([cache_v[0][:, S_LEN:], v_new], axis=1)[None]
    conv_s = cx[T_P:].reshape(N_SEQ, S_LEN, D)[:, S_LEN - 2:][None]
    return (y_prompt, y_sample, k_win_p, v_win_p, conv_p, k_win_s, v_win_s, conv_s)
```

```python
import functools

import jax
import jax.numpy as jnp
import numpy as np
from jax import lax
from jax.experimental import pallas as pl
from jax.experimental.pallas import tpu as pltpu

BF16 = jnp.bfloat16
F32 = jnp.float32
I32 = jnp.int32
U32 = jnp.uint32

D = 4096
T_P = 8192
N_SEQ = 128
S_LEN = 8
T_S = N_SEQ * S_LEN
T_ALL = T_P + T_S
PAST = 8192
HD = 64
N_KV = 8
KV_W = N_KV * HD
ROT = 16
THETA = 500000.0
WIN = 128
N_E = 32
TOP_K = 4
N_ASSIGN = T_ALL * TOP_K
LIMIT = 7.0
ALPHA = 1.702
PLE = 256
EPS = 1e-5
NEG = -1e30
N_IN = 25600

CB = 512
C_K, C_V, C_XC, C_B, C_C, C_GA, C_GC = 8, 9, 10, 18, 26, 34, 42

VMEM_LIMIT = 56 * 1024 * 1024

TM_E = 1280
NB_E = N_ASSIGN // TM_E + N_E + 1
CAP_E = NB_E * TM_E
GT = 256
SEQ_PER_STEP = 8


def _cparams(n_axes):
    return pltpu.CompilerParams(dimension_semantics=("arbitrary",) * n_axes,
                                vmem_limit_bytes=VMEM_LIMIT)


def _rms_cast_kernel(x_ref, g_ref, o_ref):
    x = x_ref[...]
    ms = jnp.mean(x * x, axis=-1, keepdims=True)
    o_ref[...] = (x * lax.rsqrt(ms + EPS) * g_ref[...]).astype(o_ref.dtype)


def _rms_norm(x, g, out_dtype, row0=0, rows=None, tm=256):
    rows = x.shape[0] if rows is None else rows
    b0 = row0 // tm
    return pl.pallas_call(
        _rms_cast_kernel,
        grid=(rows // tm,),
        in_specs=[pl.BlockSpec((tm, D), lambda i: (i + b0, 0)),
                  pl.BlockSpec((1, D), lambda i: (0, 0))],
        out_specs=pl.BlockSpec((tm, D), lambda i: (i, 0)),
        out_shape=jax.ShapeDtypeStruct((rows, D), out_dtype),
        compiler_params=_cparams(1),
        name="rms_norm",
    )(x, g.reshape(1, D))


def _rms_stacked_kernel(xp_ref, xs_ref, g_ref, o_ref, *, n_p):
    x = jnp.where(pl.program_id(0) < n_p, xp_ref[...], xs_ref[...])
    ms = jnp.mean(x * x, axis=-1, keepdims=True)
    o_ref[...] = (x * lax.rsqrt(ms + EPS) * g_ref[...]).astype(o_ref.dtype)


def _rms_norm_stacked(xp, xs, g, tm=256):
    n_p = T_P // tm
    return pl.pallas_call(
        functools.partial(_rms_stacked_kernel, n_p=n_p),
        grid=(T_ALL // tm,),
        in_specs=[pl.BlockSpec((tm, D), lambda i: (jnp.minimum(i, n_p - 1), 0)),
                  pl.BlockSpec((tm, D), lambda i: (jnp.maximum(i - n_p, 0), 0)),
                  pl.BlockSpec((1, D), lambda i: (0, 0))],
        out_specs=pl.BlockSpec((tm, D), lambda i: (i, 0)),
        out_shape=jax.ShapeDtypeStruct((T_ALL, D), BF16),
        compiler_params=_cparams(1),
        name="rms_norm_in",
    )(xp, xs, g.reshape(1, D))


def _mm_bias_kernel(lhs_ref, w_ref, b_ref, o_ref):
    acc = jnp.dot(lhs_ref[...], w_ref[...].astype(BF16), preferred_element_type=F32)
    o_ref[...] = acc + b_ref[...]


def _mm_bias_res_kernel(lhs_ref, w_ref, b_ref, rp_ref, rs_ref, o_ref, *, n_p):
    acc = jnp.dot(lhs_ref[...], w_ref[...].astype(BF16), preferred_element_type=F32)
    res = jnp.where(pl.program_id(0) < n_p, rp_ref[...], rs_ref[...])
    o_ref[...] = acc + b_ref[...] + res


def _mm_ple_kernel(lhs_ref, w_ref, h_ref, p_ref, wp_ref, o_ref):
    acc = jnp.dot(lhs_ref[...], w_ref[...].astype(BF16), preferred_element_type=F32)
    pp = jnp.dot(p_ref[...].astype(BF16), wp_ref[...].astype(BF16), preferred_element_type=F32)
    o_ref[...] = h_ref[...] + jax.nn.sigmoid(acc) * pp


def _dense(kernel, lhs, w, extras, extra_specs, tm, tn, name):
    M, K = lhs.shape
    N = w.shape[1]
    return pl.pallas_call(
        kernel,
        grid=(M // tm, N // tn),
        in_specs=[pl.BlockSpec((tm, K), lambda i, j: (i, 0)),
                  pl.BlockSpec((K, tn), lambda i, j: (0, j))] + extra_specs,
        out_specs=pl.BlockSpec((tm, tn), lambda i, j: (i, j)),
        out_shape=jax.ShapeDtypeStruct((M, N), F32),
        compiler_params=_cparams(2),
        name=name,
    )(lhs, w, *extras)


def _rope(x, c, s1, s2):
    w = x.shape[1]
    reps = w // 128
    ct = jnp.concatenate([c] * reps, axis=1) if reps > 1 else c
    s1t = jnp.concatenate([s1] * reps, axis=1) if reps > 1 else s1
    s2t = jnp.concatenate([s2] * reps, axis=1) if reps > 1 else s2
    return x * ct + pltpu.roll(x, 8, 1) * s1t + pltpu.roll(x, w - 8, 1) * s2t


def _half_placed(chunk):
    lane = lax.broadcasted_iota(I32, chunk.shape, 1)
    low = lane < HD
    other = pltpu.roll(chunk, HD, 1)
    zero = jnp.zeros_like(chunk)
    a = (jnp.where(low, chunk, zero).astype(BF16), jnp.where(low, zero, other).astype(BF16))
    b = (jnp.where(low, other, zero).astype(BF16), jnp.where(low, zero, chunk).astype(BF16))
    return a, b


def _attend_groups(c, q_rot, keys, vals, mask, sink_ref, o_ref):
    k_ab = _half_placed(keys)
    v_ab = _half_placed(vals)
    n_rows = q_rot.shape[0]
    mask4 = jnp.concatenate([mask] * 4, axis=0)
    for grp in range(2):
        q4 = jnp.concatenate([q_rot[:, 512 * grp + 128 * i:512 * grp + 128 * (i + 1)] for i in range(4)],
                             axis=0)
        q4 = (q4 * (HD ** -0.5)).astype(BF16)
        out = None
        for par in range(2):
            k2, v2 = k_ab[grp][par], v_ab[grp][par]
            sink = jnp.concatenate(
                [jnp.full((n_rows, 1), sink_ref[16 * c + 8 * grp + 2 * i + par], F32) for i in range(4)],
                axis=0)
            s = lax.dot_general(q4, k2, (((1,), (1,)), ((), ())), preferred_element_type=F32)
            s = jnp.where(mask4, s, NEG)
            m = jnp.maximum(jnp.max(s, axis=-1, keepdims=True), sink)
            p = jnp.exp(s - m)
            den = jnp.sum(p, axis=-1, keepdims=True) + jnp.exp(sink - m)
            o = jnp.dot(p.astype(BF16), v2, preferred_element_type=F32) / den
            out = o if out is None else out + o
        for i in range(4):
            col = 512 * grp + 128 * i
            o_ref[:, col:col + 128] = out[n_rows * i:n_rows * (i + 1)].astype(o_ref.dtype)


def _attn_prompt_kernel(sink_ref, q_ref, kc_ref, kp_ref, vc_ref, vp_ref,
                        cc_ref, s1c_ref, s2c_ref, cp_ref, s1p_ref, s2p_ref,
                        o_ref, krot_ref):
    b = pl.program_id(0)
    g = pl.program_id(1)
    q_rot = _rope(q_ref[...], cc_ref[...], s1c_ref[...], s2c_ref[...])
    k_cur = _rope(kc_ref[...], cc_ref[...], s1c_ref[...], s2c_ref[...])
    k_prev = _rope(kp_ref[...], cp_ref[...], s1p_ref[...], s2p_ref[...])
    krot_ref[...] = k_cur
    keys = jnp.concatenate([k_prev, k_cur], axis=0)
    vals = jnp.concatenate([vp_ref[...], vc_ref[...]], axis=0)
    r = lax.broadcasted_iota(I32, (128, 256), 0)
    c = lax.broadcasted_iota(I32, (128, 256), 1)
    mask = (c > r) & (c <= r + WIN) & ((b > 0) | (c >= 128))
    _attend_groups(g, q_rot, keys, vals, mask, sink_ref, o_ref)


def _attn_sample_kernel(sink_ref, q_ref, kn_ref, vn_ref, ck_ref, cv_ref,
                        c_ref, s1_ref, s2_ref, mask_ref, o_ref, krot_ref):
    g = pl.program_id(1)
    q_rot = _rope(q_ref[...], c_ref[...], s1_ref[...], s2_ref[...])
    k_new = _rope(kn_ref[...], c_ref[...], s1_ref[...], s2_ref[...])
    krot_ref[...] = k_new
    keys = jnp.concatenate([ck_ref[...], k_new], axis=0)
    vals = jnp.concatenate([cv_ref[...], vn_ref[...]], axis=0)
    mask = mask_ref[...] > 0.5
    _attend_groups(g, q_rot, keys, vals, mask, sink_ref, o_ref)


def _rope_tables(pos):
    half = ROT // 2
    inv_freq = THETA ** (-jnp.arange(half, dtype=F32) / half)
    ang = pos.astype(F32)[:, None] * inv_freq[None, :]
    cos, sin = jnp.cos(ang), jnp.sin(ang)
    n = pos.shape[0]
    ones = jnp.ones((n, HD - ROT), F32)
    zeros = jnp.zeros((n, HD - ROT), F32)
    z8 = jnp.zeros((n, half), F32)
    c = jnp.concatenate([cos, cos, ones], axis=1)
    s1 = jnp.concatenate([z8, sin, zeros], axis=1)
    s2 = jnp.concatenate([-sin, z8, zeros], axis=1)
    return tuple(jnp.concatenate([t, t], axis=1) for t in (c, s1, s2))


def _sample_mask():
    sb = SEQ_PER_STEP
    r = np.arange(sb * S_LEN)
    sq, qi = r // S_LEN, r % S_LEN
    c = np.arange(sb * WIN)
    m_cache = (c[None, :] // WIN == sq[:, None]) & (c[None, :] % WIN > qi[:, None])
    cn = np.arange(sb * S_LEN)
    m_new = (cn[None, :] // S_LEN == sq[:, None]) & (cn[None, :] % S_LEN <= qi[:, None])
    return np.concatenate([m_cache, m_new], axis=1).astype(np.float32)


def _attention(z, cache_k, cache_v, sinks):
    smem = pl.BlockSpec(memory_space=pltpu.SMEM)
    tabs_p = _rope_tables(jnp.arange(T_P))
    prev = lambda b: jnp.maximum(b - 1, 0)
    ck0, cv0 = C_K * CB // 128, C_V * CB // 128
    tab_cur = pl.BlockSpec((128, 128), lambda b, g: (b, 0))
    tab_prev = pl.BlockSpec((128, 128), lambda b, g: (prev(b), 0))
    y_p, krot_p = pl.pallas_call(
        _attn_prompt_kernel,
        grid=(T_P // 128, N_KV // 2),
        in_specs=[smem,
                  pl.BlockSpec((128, 2 * CB), lambda b, g: (b, g)),
                  pl.BlockSpec((128, 128), lambda b, g: (b, ck0 + g)),
                  pl.BlockSpec((128, 128), lambda b, g: (prev(b), ck0 + g)),
                  pl.BlockSpec((128, 128), lambda b, g: (b, cv0 + g)),
                  pl.BlockSpec((128, 128), lambda b, g: (prev(b), cv0 + g)),
                  tab_cur, tab_cur, tab_cur, tab_prev, tab_prev, tab_prev],
        out_specs=[pl.BlockSpec((128, 2 * CB), lambda b, g: (b, g)),
                   pl.BlockSpec((128, 128), lambda b, g: (b, g))],
        out_shape=[jax.ShapeDtypeStruct((T_P, D), BF16),
                   jax.ShapeDtypeStruct((T_P, KV_W), F32)],
        compiler_params=_cparams(2),
        name="attn_prompt",
    )(sinks, z, z, z, z, z, *tabs_p, *tabs_p)

    sb = SEQ_PER_STEP
    rows = sb * S_LEN
    tabs_s = _rope_tables(PAST + (jnp.arange(rows) % S_LEN))
    mask = jnp.asarray(_sample_mask())
    nk = sb * WIN + rows
    rb0 = T_P // rows
    const = lambda shape: pl.BlockSpec(shape, lambda b, g: (0, 0))
    y_s, krot_s = pl.pallas_call(
        _attn_sample_kernel,
        grid=(N_SEQ // sb, N_KV // 2),
        in_specs=[smem,
                  pl.BlockSpec((rows, 2 * CB), lambda b, g: (b + rb0, g)),
                  pl.BlockSpec((rows, 128), lambda b, g: (b + rb0, ck0 + g)),
                  pl.BlockSpec((rows, 128), lambda b, g: (b + rb0, cv0 + g)),
                  pl.BlockSpec((sb * WIN, 128), lambda b, g: (b, g)),
                  pl.BlockSpec((sb * WIN, 128), lambda b, g: (b, g)),
                  const((rows, 128)), const((rows, 128)), const((rows, 128)),
                  const((rows, nk))],
        out_specs=[pl.BlockSpec((rows, 2 * CB), lambda b, g: (b, g)),
                   pl.BlockSpec((rows, 128), lambda b, g: (b, g))],
        out_shape=[jax.ShapeDtypeStruct((T_S, D), BF16),
                   jax.ShapeDtypeStruct((T_S, KV_W), F32)],
        compiler_params=_cparams(2),
        name="attn_sample",
    )(sinks, z, z, z, cache_k.reshape(N_SEQ * WIN, KV_W), cache_v.reshape(N_SEQ * WIN, KV_W),
      *tabs_s, mask)
    return y_p, krot_p, y_s, krot_s


MT = 512


def _merge_kernel(yp_ref, ys_ref, xc_ref, bg_ref, cg_ref, ga_ref, gc_ref, xc8_ref, cg8_ref,
                  s1_ref, s2_ref, cw_ref, m_ref, cx_ref):
    i = pl.program_id(1)
    n_p = T_P // MT
    is_sample = i >= n_p
    y_att = jnp.where(is_sample, ys_ref[...], yp_ref[...]).astype(F32)
    cx = cg_ref[...] * xc_ref[...]
    cx_ref[...] = cx
    row = lax.broadcasted_iota(I32, cx.shape, 0)
    r1 = pltpu.roll(cx, 1, 0)
    r2 = pltpu.roll(cx, 2, 0)
    p8 = cg8_ref[...] * xc8_ref[...] * jnp.where(i > 0, 1.0, 0.0)
    pad = jnp.zeros((MT - 8, cx.shape[1]), F32)
    p1 = jnp.concatenate([pltpu.roll(p8, 1, 0), pad], axis=0)
    p2 = jnp.concatenate([pltpu.roll(p8, 2, 0), pad], axis=0)
    c1_p = jnp.where(row >= 1, r1, p1)
    c2_p = jnp.where(row >= 2, r2, p2)
    c1_s = jnp.where(row % S_LEN >= 1, r1, s1_ref[...])
    c2_s = jnp.where(row % S_LEN >= 2, r2, s2_ref[...])
    c1 = jnp.where(is_sample, c1_s, c1_p)
    c2 = jnp.where(is_sample, c2_s, c2_p)
    w = cw_ref[...]
    y_conv = bg_ref[...] * (w[0:1] * c2 + w[1:2] * c1 + w[2:3] * cx)
    merged = jax.nn.sigmoid(ga_ref[...]) * y_att + jax.nn.sigmoid(gc_ref[...]) * y_conv
    m_ref[...] = merged.astype(m_ref.dtype)


def _merge(z, y_p, y_s, state_conv, conv_w):
    n_p = T_P // MT
    zb = lambda off: pl.BlockSpec((MT, MT), lambda j, i: (i, off + j))
    z8 = lambda off: pl.BlockSpec((8, MT), lambda j, i: (jnp.maximum(i * (MT // 8) - 1, 0), off + j))
    sblk = pl.BlockSpec((MT, MT), lambda j, i: (jnp.maximum(i - n_p, 0), j))
    st = state_conv
    s1 = jnp.concatenate([st[:, 1:2], jnp.zeros((N_SEQ, S_LEN - 1, D), F32)], axis=1).reshape(T_S, D)
    s2 = jnp.concatenate([st[:, 0:1], st[:, 1:2], jnp.zeros((N_SEQ, S_LEN - 2, D), F32)], axis=1).reshape(T_S, D)
    cw = jnp.concatenate([conv_w, jnp.zeros((5, D), F32)], axis=0)
    return pl.pallas_call(
        _merge_kernel,
        grid=(D // MT, T_ALL // MT),
        in_specs=[pl.BlockSpec((MT, MT), lambda j, i: (jnp.minimum(i, n_p - 1), j)),
                  sblk,
                  zb(C_XC), zb(C_B), zb(C_C), zb(C_GA), zb(C_GC), z8(C_XC), z8(C_C),
                  sblk, sblk,
                  pl.BlockSpec((8, MT), lambda j, i: (0, j))],
        out_specs=[pl.BlockSpec((MT, MT), lambda j, i: (i, j)),
                   pl.BlockSpec((MT, MT), lambda j, i: (i, j))],
        out_shape=[jax.ShapeDtypeStruct((T_ALL, D), BF16),
                   jax.ShapeDtypeStruct((T_ALL, D), F32)],
        compiler_params=_cparams(2),
        name="merge",
    )(y_p, y_s, z, z, z, z, z, z, z, s1, s2, cw)


RT = 256


def _pack_bf16_pair(lo, hi):
    lo_bits = pltpu.bitcast(lo.astype(BF16).astype(F32), U32)
    hi_bits = pltpu.bitcast(hi.astype(BF16).astype(F32), U32)
    return (lo_bits >> 16) | (hi_bits & jnp.uint32(0xFFFF0000))


def _unpack_bf16_pair(w):
    lo = pltpu.bitcast(w << 16, F32)
    hi = pltpu.bitcast(w & jnp.uint32(0xFFFF0000), F32)
    return lo, hi


def _router_kernel(h_ref, g_ref, wr_ref, br_ref, tri_ref, hn_ref, ti_ref, tg_ref, rk_ref, cnt_ref,
                   base_ref):
    @pl.when(pl.program_id(0) == 0)
    def _():
        base_ref[...] = jnp.zeros_like(base_ref)

    x = h_ref[...]
    ms = jnp.mean(x * x, axis=-1, keepdims=True)
    hn = x * lax.rsqrt(ms + EPS) * g_ref[...]
    hn_ref[...] = _pack_bf16_pair(hn[:, :D // 2], hn[:, D // 2:])
    w = wr_ref[...]
    h_hi = hn.astype(BF16)
    h_lo = (hn - h_hi.astype(F32)).astype(BF16)
    w_hi = w.astype(BF16)
    w_lo = (w - w_hi.astype(F32)).astype(BF16)
    logits = (jnp.dot(h_hi, w_hi, preferred_element_type=F32)
              + jnp.dot(h_hi, w_lo, preferred_element_type=F32)
              + jnp.dot(h_lo, w_hi, preferred_element_type=F32)) + br_ref[...]
    lane = lax.broadcasted_iota(I32, logits.shape, 1)
    vals = jnp.where(lane < N_E, logits, NEG)
    tops, idxs = [], []
    for _ in range(TOP_K):
        m = jnp.max(vals, axis=-1, keepdims=True)
        idx = jnp.min(jnp.where(vals == m, lane, 128), axis=-1, keepdims=True)
        tops.append(m)
        idxs.append(idx)
        vals = jnp.where(lane == idx, NEG, vals)
    es = [jnp.exp(t - tops[0]) for t in tops]
    den = es[0] + es[1] + es[2] + es[3]
    onehot = jnp.zeros(logits.shape, F32)
    for k in range(TOP_K):
        onehot = onehot + jnp.where(lane == idxs[k], 1.0, 0.0)
    before = jnp.dot(tri_ref[...], onehot.astype(BF16), preferred_element_type=F32) + base_ref[0:1, :]
    ti = jnp.zeros(logits.shape, I32)
    tg = jnp.zeros(logits.shape, F32)
    rk = jnp.zeros(logits.shape, F32)
    for k in range(TOP_K):
        ti = jnp.where(lane == k, idxs[k], ti)
        tg = jnp.where(lane == k, es[k] / den, tg)
        r = jnp.sum(jnp.where(lane == idxs[k], before, 0.0), axis=-1, keepdims=True)
        rk = jnp.where(lane == k, r, rk)
    ti_ref[...] = ti
    tg_ref[...] = tg
    rk_ref[...] = rk.astype(I32)
    total = base_ref[...] + jnp.sum(onehot, axis=0, keepdims=True)
    base_ref[...] = total
    cnt_ref[...] = total.astype(I32)


def _router(h1, g_ffn, w_router, b_router):
    wr = jnp.pad(w_router, ((0, 0), (0, 128 - N_E)))
    br = jnp.pad(b_router, (0, 128 - N_E)).reshape(1, 128)
    tri = jnp.asarray(np.tril(np.ones((RT, RT), np.float32), -1), BF16)
    blk128 = pl.BlockSpec((RT, 128), lambda i: (i, 0))
    return pl.pallas_call(
        _router_kernel,
        grid=(T_ALL // RT,),
        in_specs=[pl.BlockSpec((RT, D), lambda i: (i, 0)),
                  pl.BlockSpec((1, D), lambda i: (0, 0)),
                  pl.BlockSpec((D, 128), lambda i: (0, 0)),
                  pl.BlockSpec((1, 128), lambda i: (0, 0)),
                  pl.BlockSpec((RT, RT), lambda i: (0, 0))],
        out_specs=[pl.BlockSpec((RT, D // 2), lambda i: (i, 0)),
                   blk128, blk128, blk128,
                   pl.BlockSpec((8, 128), lambda i: (0, 0))],
        out_shape=[jax.ShapeDtypeStruct((T_ALL, D // 2), U32),
                   jax.ShapeDtypeStruct((T_ALL, 128), I32),
                   jax.ShapeDtypeStruct((T_ALL, 128), F32),
                   jax.ShapeDtypeStruct((T_ALL, 128), I32),
                   jax.ShapeDtypeStruct((8, 128), I32)],
        scratch_shapes=[pltpu.VMEM((8, 128), F32)],
        compiler_params=_cparams(1),
        name="router",
    )(h1, g_ffn.reshape(1, D), wr, br, tri)


def _row_copy(src_hbm, dst, sem, src_row, dst_row):
    return pltpu.make_async_copy(src_hbm.at[pl.ds(src_row, 1)], dst.at[pl.ds(dst_row, 1)], sem)


GU_TN = 512
GU_NJ = 2 * D // GU_TN
GU_ROWS = TM_E // GU_NJ


def _swiglu_pairs(gu):
    outs = []
    for q in range(2):
        a = gu[:, 256 * q:256 * q + 128]
        b = gu[:, 256 * q + 128:256 * q + 256]
        lane = lax.broadcasted_iota(I32, a.shape, 1)
        even = (lane & 1) == 0
        glu = jnp.where(even, a, pltpu.roll(b, 1, 1))
        lin = jnp.where(even, pltpu.roll(a, 127, 1), b)
        glu = jnp.minimum(glu, LIMIT)
        lin = jnp.clip(lin, -LIMIT, LIMIT)
        outs.append(glu * jax.nn.sigmoid(ALPHA * glu) * (lin + 1.0))
    return jnp.concatenate(outs, axis=1)


def _pair_perm():
    p = np.zeros((GU_TN // 2, GU_TN // 2), np.float32)
    for q in range(GU_TN // 256):
        for j in range(HD):
            for b in range(2):
                p[128 * q + 2 * j + b, 128 * q + HD * b + j] = 1.0
    return p


def _gu_kernel(e_ref, tok0_ref, tokn_ref, hn_hbm, w_ref, b_ref, p_ref, o_ref,
               stage_ref, lhs_ref, sem):
    k = pl.program_id(0)
    j = pl.program_id(1)

    def fetch(tok_ref, r0, n):
        for r in range(n):
            _row_copy(hn_hbm, stage_ref, sem, tok_ref[0, r0 + r], r0 + r).start(priority=1)

    def wait_block():
        pltpu.make_async_copy(hn_hbm.at[pl.ds(0, TM_E)], stage_ref, sem).wait()

    @pl.when(j == 0)
    def _():
        @pl.when(k == 0)
        def _():
            def body(c, carry):
                fetch(tok0_ref, c * GU_ROWS, GU_ROWS)
                return carry
            lax.fori_loop(0, GU_NJ, body, 0)

        wait_block()

        def body(c, carry):
            rows = pl.ds(pl.multiple_of(c * 128, 128), 128)
            lo, hi = _unpack_bf16_pair(stage_ref[rows, :])
            lhs_ref[rows, 0:D // 2] = lo.astype(BF16)
            lhs_ref[rows, D // 2:D] = hi.astype(BF16)
            return carry
        lax.fori_loop(0, TM_E // 128, body, 0)

    fetch(tokn_ref, j * GU_ROWS, GU_ROWS)
    gu = jnp.dot(lhs_ref[...], w_ref[...].astype(BF16), preferred_element_type=F32) + b_ref[...]
    act = _swiglu_pairs(gu).astype(BF16)
    o_ref[...] = jnp.dot(act, p_ref[...], preferred_element_type=F32).astype(BF16)

    @pl.when((k == pl.num_programs(0) - 1) & (j == GU_NJ - 1))
    def _():
        wait_block()


def _moe_gate_up(hn_packed, tok_of_slot, w, b, blk_e, n_used):
    tn_out = GU_TN // 2
    tok3 = tok_of_slot.reshape(NB_E, 1, TM_E)
    grid_spec = pltpu.PrefetchScalarGridSpec(
        num_scalar_prefetch=1,
        grid=(n_used, GU_NJ),
        in_specs=[pl.BlockSpec((None, 1, TM_E), lambda k, j, e: (0, 0, 0), memory_space=pltpu.SMEM),
                  pl.BlockSpec((None, 1, TM_E), lambda k, j, e: (jnp.minimum(k + 1, NB_E - 1), 0, 0),
                               memory_space=pltpu.SMEM),
                  pl.BlockSpec(memory_space=pl.ANY),
                  pl.BlockSpec((None, D, GU_TN), lambda k, j, e: (e[k], 0, j)),
                  pl.BlockSpec((None, 1, GU_TN), lambda k, j, e: (e[k], 0, j)),
                  pl.BlockSpec((tn_out, tn_out), lambda k, j, e: (0, 0))],
        out_specs=pl.BlockSpec((TM_E, tn_out), lambda k, j, e: (k, j)),
        scratch_shapes=[pltpu.VMEM((TM_E, D // 2), U32), pltpu.VMEM((TM_E, D), BF16),
                        pltpu.SemaphoreType.DMA(())],
    )
    return pl.pallas_call(
        _gu_kernel,
        grid_spec=grid_spec,
        out_shape=jax.ShapeDtypeStruct((CAP_E, GU_NJ * tn_out), BF16),
        compiler_params=_cparams(2),
        name="moe_gate_up",
    )(blk_e, tok3, tok3, hn_packed, w, b.reshape(N_E, 1, 2 * D), jnp.asarray(_pair_perm(), BF16))


DN_TN = 512


def _down_kernel(e_ref, a_ref, w_ref, b_ref, o_ref):
    y = jnp.dot(a_ref[...], w_ref[...].astype(BF16), preferred_element_type=F32) + b_ref[...]
    o_ref[...] = _pack_bf16_pair(y[:, :DN_TN // 2], y[:, DN_TN // 2:])


def _moe_down(act, w, b, blk_e, n_used):
    grid_spec = pltpu.PrefetchScalarGridSpec(
        num_scalar_prefetch=1,
        grid=(n_used, D // DN_TN),
        in_specs=[pl.BlockSpec((TM_E, D), lambda k, j, e: (k, 0)),
                  pl.BlockSpec((None, D, DN_TN), lambda k, j, e: (e[k], 0, j)),
                  pl.BlockSpec((None, 1, DN_TN), lambda k, j, e: (e[k], 0, j))],
        out_specs=pl.BlockSpec((TM_E, DN_TN // 2), lambda k, j, e: (k, j)),
    )
    return pl.pallas_call(
        _down_kernel,
        grid_spec=grid_spec,
        out_shape=jax.ShapeDtypeStruct((CAP_E, D // 2), U32),
        compiler_params=_cparams(2),
        name="moe_down",
    )(blk_e, act, w, b.reshape(N_E, 1, D))


def _combine_kernel(slot_ref, slotn_ref, g_ref, h_ref, gp_ref, y_hbm, h2_ref, hn_ref, buf_ref, sems):
    i = pl.program_id(0)
    n = pl.num_programs(0)
    cur = i % 2

    def fetch(idx_ref, slot):
        def body(c, carry):
            for r in range(8):
                row = c * 8 + r
                for k in range(TOP_K):
                    _row_copy(y_hbm, buf_ref.at[slot, k], sems.at[slot],
                              idx_ref[0, TOP_K * row + k], row).start(priority=k % 2)
            return carry
        lax.fori_loop(0, GT // 8, body, 0)

    @pl.when(i == 0)
    def _():
        fetch(slot_ref, 0)

    @pl.when(i + 1 < n)
    def _():
        fetch(slotn_ref, 1 - cur)

    for k in range(TOP_K):
        pltpu.make_async_copy(y_hbm.at[pl.ds(0, GT)], buf_ref.at[cur, k], sems.at[cur]).wait()

    g = g_ref[...]
    tw = DN_TN // 2
    for c in range(D // DN_TN):
        lo_acc = h_ref[:, DN_TN * c:DN_TN * c + tw]
        hi_acc = h_ref[:, DN_TN * c + tw:DN_TN * (c + 1)]
        for k in range(TOP_K):
            lo, hi = _unpack_bf16_pair(buf_ref[cur, k, :, tw * c:tw * (c + 1)])
            lo_acc = lo_acc + g[:, k:k + 1] * lo
            hi_acc = hi_acc + g[:, k:k + 1] * hi
        h2_ref[:, DN_TN * c:DN_TN * c + tw] = lo_acc
        h2_ref[:, DN_TN * c + tw:DN_TN * (c + 1)] = hi_acc
    h2 = h2_ref[...]
    ms = jnp.mean(h2 * h2, axis=-1, keepdims=True)
    hn_ref[...] = (h2 * lax.rsqrt(ms + EPS) * gp_ref[...]).astype(BF16)


def _moe_combine(slot, gates, h1, g_ple, y_packed):
    n = T_ALL // GT
    slot3 = slot.reshape(n, 1, GT * TOP_K)
    return pl.pallas_call(
        _combine_kernel,
        grid=(n,),
        in_specs=[pl.BlockSpec((None, 1, GT * TOP_K), lambda i: (i, 0, 0), memory_space=pltpu.SMEM),
                  pl.BlockSpec((None, 1, GT * TOP_K), lambda i: (jnp.minimum(i + 1, n - 1), 0, 0),
                               memory_space=pltpu.SMEM),
                  pl.BlockSpec((GT, 128), lambda i: (i, 0)),
                  pl.BlockSpec((GT, D), lambda i: (i, 0)),
                  pl.BlockSpec((1, D), lambda i: (0, 0)),
                  pl.BlockSpec(memory_space=pl.ANY)],
        out_specs=[pl.BlockSpec((GT, D), lambda i: (i, 0)),
                   pl.BlockSpec((GT, D), lambda i: (i, 0))],
        out_shape=[jax.ShapeDtypeStruct((T_ALL, D), F32),
                   jax.ShapeDtypeStruct((T_ALL, D), BF16)],
        scratch_shapes=[pltpu.VMEM((2, TOP_K, GT, D // 2), U32), pltpu.SemaphoreType.DMA((2,))],
        compiler_params=_cparams(1),
        name="moe_combine",
    )(slot3, slot3, gates, h1, g_ple.reshape(1, D), y_packed)


def _routing_tables(top_i, rank, counts):
    flat_e = top_i.reshape(-1)
    nblk = (counts + TM_E - 1) // TM_E
    blk_end = jnp.cumsum(nblk)
    blk_start = blk_end - nblk
    slot = (blk_start[flat_e] * TM_E + rank.reshape(-1)).astype(I32)
    n_used = blk_end[-1].astype(I32)
    ks = jnp.arange(NB_E, dtype=I32)
    blk_e = jnp.minimum(jnp.searchsorted(blk_end, ks, side="right"), N_E - 1).astype(I32)
    flat_t = jnp.repeat(jnp.arange(T_ALL, dtype=I32), TOP_K)
    tok_of_slot = jnp.zeros((CAP_E,), I32).at[slot].set(flat_t)
    return slot, tok_of_slot, blk_e, n_used


def kernel(x_prompt, x_sample, cache_k, cache_v, state_conv, p_prompt, p_sample, g_mix, w_in, b_in, conv_w, sinks, w_out, b_out, g_ffn, w_router, b_router, w_gate_up, b_gate_up, w_down, b_down, g_ple, w_ple_gate, w_ple, g_final):
    xp = x_prompt[0]
    xs = x_sample.reshape(T_S, D)
    p_all = jnp.concatenate([p_prompt[0, 0], p_sample[0].reshape(T_S, PLE)], axis=0)

    u = _rms_norm_stacked(xp, xs, g_mix[0])
    z = _dense(_mm_bias_kernel, u, w_in[0], [b_in[0].reshape(1, N_IN)],
               [pl.BlockSpec((1, 512), lambda i, j: (0, j))], 1024, 512, "in_proj")

    y_p, krot_p, y_s, krot_s = _attention(z, cache_k[0], cache_v[0], sinks[0])

    merged, cx = _merge(z, y_p, y_s, state_conv[0], conv_w[0])
    n_p = T_P // 1024
    h1 = _dense(functools.partial(_mm_bias_res_kernel, n_p=n_p), merged, w_out[0],
                [b_out[0].reshape(1, D), xp, xs],
                [pl.BlockSpec((1, 512), lambda i, j: (0, j)),
                 pl.BlockSpec((1024, 512), lambda i, j: (jnp.minimum(i, n_p - 1), j)),
                 pl.BlockSpec((1024, 512), lambda i, j: (jnp.maximum(i - n_p, 0), j))],
                1024, 512, "out_proj")

    hn_packed, top_i, top_g, rank, counts = _router(h1, g_ffn[0], w_router[0], b_router[0])
    slot, tok_of_slot, blk_e, n_used = _routing_tables(
        top_i[:, :TOP_K], rank[:, :TOP_K], counts[0, :N_E])
    act = _moe_gate_up(hn_packed, tok_of_slot, w_gate_up[0], b_gate_up[0], blk_e, n_used)
    y_packed = _moe_down(act, w_down[0], b_down[0], blk_e, n_used)
    h2, hn2 = _moe_combine(slot, top_g, h1, g_ple[0], y_packed)

    h3 = _dense(_mm_ple_kernel, hn2, w_ple_gate[0], [h2, p_all, w_ple[0]],
                [pl.BlockSpec((1024, 512), lambda i, j: (i, j)),
                 pl.BlockSpec((1024, PLE), lambda i, j: (i, 0)),
                 pl.BlockSpec((PLE, 512), lambda i, j: (0, j))], 1024, 512, "ple")

    y_prompt = _rms_norm(h3, g_final, F32, 0, T_P).reshape(1, T_P, D)
    y_sample = _rms_norm(h3, g_final, F32, T_P, T_S).reshape(N_SEQ, S_LEN, D)

    w_p = min(WIN, T_P)
    k_win_p = krot_p[T_P - w_p:].reshape(1, 1, w_p, N_KV, HD)
    v_win_p = z[T_P - w_p:T_P, C_V * CB:C_V * CB + KV_W].reshape(1, 1, w_p, N_KV, HD)
    conv_p = cx[T_P - 2:T_P].reshape(1, 1, 2, D)
    k_new = krot_s.reshape(N_SEQ, S_LEN, N_KV, HD)
    v_new = z[T_P:, C_V * CB:C_V * CB + KV_W].reshape(N_SEQ, S_LEN, N_KV, HD)
    k_win_s = jnp.concatenate([cache_k[0][:, S_LEN:], k_new], axis=1)[None]
    v_win_s = jnp.concatenate([cache_v[0][:, S_LEN:], v_new], axis=1)[None]
    conv_s = cx[T_P:].reshape(N_SEQ, S_LEN, D)[:, S_LEN - 2:][None]
    return (y_prompt, y_sample, k_win_p, v_win_p, conv_p, k_win_s, v_win_s, conv_s)
```

```python
import functools

import jax
import jax.numpy as jnp
import numpy as np
from jax import lax
from jax.experimental import pallas as pl
from jax.experimental.pallas import tpu as pltpu

BF16 = jnp.bfloat16
F32 = jnp.float32
I32 = jnp.int32
U32 = jnp.uint32

D = 4096
T_P = 8192
N_SEQ = 128
S_LEN = 8
T_S = N_SEQ * S_LEN
T_ALL = T_P + T_S
PAST = 8192
HD = 64
N_KV = 8
KV_W = N_KV * HD
ROT = 16
THETA = 500000.0
WIN = 128
N_E = 32
TOP_K = 4
N_ASSIGN = T_ALL * TOP_K
LIMIT = 7.0
ALPHA = 1.702
PLE = 256
EPS = 1e-5
NEG = -1e30
N_IN = 25600

CB = 512
W_Q, W_K, W_V, W_XC, W_B, W_C, W_GA, W_GC = 0, 8, 9, 10, 18, 26, 34, 42
Z32_K, Z32_V, Z32_XC, Z32_C, Z32_TILES = 0, 1, 2, 10, 18
Z16_Q, Z16_B, Z16_GA, Z16_GC, Z16_TILES = 0, 8, 16, 24, 32


def _z32_col(j):
    return jnp.where(j < Z32_C, W_K + j, W_C + (j - Z32_C))


def _z16_col(j):
    return jnp.where(j < Z16_B, W_Q + j, jnp.where(j < Z16_GA, W_B + (j - Z16_B), W_GA + (j - Z16_GA)))

VMEM_LIMIT = 56 * 1024 * 1024

TM_E = 1280
NB_E = N_ASSIGN // TM_E + N_E + 1
CAP_E = NB_E * TM_E
GT = 256
SEQ_PER_STEP = 8


def _cparams(n_axes):
    return pltpu.CompilerParams(dimension_semantics=("arbitrary",) * n_axes,
                                vmem_limit_bytes=VMEM_LIMIT)


def _rms_cast_kernel(x_ref, g_ref, o_ref):
    x = x_ref[...]
    ms = jnp.mean(x * x, axis=-1, keepdims=True)
    o_ref[...] = (x * lax.rsqrt(ms + EPS) * g_ref[...]).astype(o_ref.dtype)


def _rms_norm(x, g, out_dtype, row0=0, rows=None, tm=256):
    rows = x.shape[0] if rows is None else rows
    b0 = row0 // tm
    return pl.pallas_call(
        _rms_cast_kernel,
        grid=(rows // tm,),
        in_specs=[pl.BlockSpec((tm, D), lambda i: (i + b0, 0)),
                  pl.BlockSpec((1, D), lambda i: (0, 0))],
        out_specs=pl.BlockSpec((tm, D), lambda i: (i, 0)),
        out_shape=jax.ShapeDtypeStruct((rows, D), out_dtype),
        compiler_params=_cparams(1),
        name="rms_norm",
    )(x, g.reshape(1, D))


def _rms_stacked_kernel(xp_ref, xs_ref, g_ref, o_ref, *, n_p):
    x = jnp.where(pl.program_id(0) < n_p, xp_ref[...], xs_ref[...])
    ms = jnp.mean(x * x, axis=-1, keepdims=True)
    o_ref[...] = (x * lax.rsqrt(ms + EPS) * g_ref[...]).astype(o_ref.dtype)


def _rms_norm_stacked(xp, xs, g, tm=256):
    n_p = T_P // tm
    return pl.pallas_call(
        functools.partial(_rms_stacked_kernel, n_p=n_p),
        grid=(T_ALL // tm,),
        in_specs=[pl.BlockSpec((tm, D), lambda i: (jnp.minimum(i, n_p - 1), 0)),
                  pl.BlockSpec((tm, D), lambda i: (jnp.maximum(i - n_p, 0), 0)),
                  pl.BlockSpec((1, D), lambda i: (0, 0))],
        out_specs=pl.BlockSpec((tm, D), lambda i: (i, 0)),
        out_shape=jax.ShapeDtypeStruct((T_ALL, D), BF16),
        compiler_params=_cparams(1),
        name="rms_norm_in",
    )(xp, xs, g.reshape(1, D))


def _mm_bias_kernel(lhs_ref, w_ref, b_ref, o_ref):
    acc = jnp.dot(lhs_ref[...], w_ref[...].astype(BF16), preferred_element_type=F32)
    o_ref[...] = (acc + b_ref[...]).astype(o_ref.dtype)


def _mm_bias_res_kernel(lhs_ref, w_ref, b_ref, rp_ref, rs_ref, o_ref, *, n_p):
    acc = jnp.dot(lhs_ref[...], w_ref[...].astype(BF16), preferred_element_type=F32)
    res = jnp.where(pl.program_id(0) < n_p, rp_ref[...], rs_ref[...])
    o_ref[...] = acc + b_ref[...] + res


def _mm_ple_kernel(lhs_ref, w_ref, h_ref, p_ref, wp_ref, o_ref):
    acc = jnp.dot(lhs_ref[...], w_ref[...].astype(BF16), preferred_element_type=F32)
    pp = jnp.dot(p_ref[...].astype(BF16), wp_ref[...].astype(BF16), preferred_element_type=F32)
    o_ref[...] = h_ref[...] + jax.nn.sigmoid(acc) * pp


def _dense(kernel, lhs, w, extras, extra_specs, tm, tn, name, col_map=None, n_tiles=None,
           out_dtype=F32):
    M, K = lhs.shape
    n_tiles = w.shape[1] // tn if n_tiles is None else n_tiles
    col_map = (lambda j: j) if col_map is None else col_map
    return pl.pallas_call(
        kernel,
        grid=(M // tm, n_tiles),
        in_specs=[pl.BlockSpec((tm, K), lambda i, j: (i, 0)),
                  pl.BlockSpec((K, tn), lambda i, j: (0, col_map(j)))] + extra_specs,
        out_specs=pl.BlockSpec((tm, tn), lambda i, j: (i, j)),
        out_shape=jax.ShapeDtypeStruct((M, n_tiles * tn), out_dtype),
        compiler_params=_cparams(2),
        name=name,
    )(lhs, w, *extras)


def _rope(x, c, s1, s2):
    w = x.shape[1]
    reps = w // 128
    ct = jnp.concatenate([c] * reps, axis=1) if reps > 1 else c
    s1t = jnp.concatenate([s1] * reps, axis=1) if reps > 1 else s1
    s2t = jnp.concatenate([s2] * reps, axis=1) if reps > 1 else s2
    return x * ct + pltpu.roll(x, 8, 1) * s1t + pltpu.roll(x, w - 8, 1) * s2t


def _half_placed(chunk):
    lane = lax.broadcasted_iota(I32, chunk.shape, 1)
    low = lane < HD
    other = pltpu.roll(chunk, HD, 1)
    zero = jnp.zeros_like(chunk)
    a = (jnp.where(low, chunk, zero).astype(BF16), jnp.where(low, zero, other).astype(BF16))
    b = (jnp.where(low, other, zero).astype(BF16), jnp.where(low, zero, chunk).astype(BF16))
    return a, b


def _attend_groups(c, q_rot, keys, vals, mask, sink_ref, o_ref):
    k_ab = _half_placed(keys)
    v_ab = _half_placed(vals)
    n_rows = q_rot.shape[0]
    mask4 = jnp.concatenate([mask] * 4, axis=0)
    for grp in range(2):
        q4 = jnp.concatenate([q_rot[:, 512 * grp + 128 * i:512 * grp + 128 * (i + 1)] for i in range(4)],
                             axis=0)
        q4 = (q4 * (HD ** -0.5)).astype(BF16)
        out = None
        for par in range(2):
            k2, v2 = k_ab[grp][par], v_ab[grp][par]
            sink = jnp.concatenate(
                [jnp.full((n_rows, 1), sink_ref[16 * c + 8 * grp + 2 * i + par], F32) for i in range(4)],
                axis=0)
            s = lax.dot_general(q4, k2, (((1,), (1,)), ((), ())), preferred_element_type=F32)
            s = jnp.where(mask4, s, NEG)
            m = jnp.maximum(jnp.max(s, axis=-1, keepdims=True), sink)
            p = jnp.exp(s - m)
            den = jnp.sum(p, axis=-1, keepdims=True) + jnp.exp(sink - m)
            o = jnp.dot(p.astype(BF16), v2, preferred_element_type=F32) / den
            out = o if out is None else out + o
        for i in range(4):
            col = 512 * grp + 128 * i
            o_ref[:, col:col + 128] = out[n_rows * i:n_rows * (i + 1)].astype(o_ref.dtype)


def _attn_prompt_kernel(sink_ref, q_ref, kc_ref, kp_ref, vc_ref, vp_ref,
                        cc_ref, s1c_ref, s2c_ref, cp_ref, s1p_ref, s2p_ref,
                        o_ref, krot_ref):
    b = pl.program_id(0)
    g = pl.program_id(1)
    q_rot = _rope(q_ref[...].astype(F32), cc_ref[...], s1c_ref[...], s2c_ref[...])
    k_cur = _rope(kc_ref[...], cc_ref[...], s1c_ref[...], s2c_ref[...])
    k_prev = _rope(kp_ref[...], cp_ref[...], s1p_ref[...], s2p_ref[...])
    krot_ref[...] = k_cur
    keys = jnp.concatenate([k_prev, k_cur], axis=0)
    vals = jnp.concatenate([vp_ref[...], vc_ref[...]], axis=0)
    r = lax.broadcasted_iota(I32, (128, 256), 0)
    c = lax.broadcasted_iota(I32, (128, 256), 1)
    mask = (c > r) & (c <= r + WIN) & ((b > 0) | (c >= 128))
    _attend_groups(g, q_rot, keys, vals, mask, sink_ref, o_ref)


def _attn_sample_kernel(sink_ref, q_ref, kn_ref, vn_ref, ck_ref, cv_ref,
                        c_ref, s1_ref, s2_ref, mask_ref, o_ref, krot_ref):
    g = pl.program_id(1)
    q_rot = _rope(q_ref[...].astype(F32), c_ref[...], s1_ref[...], s2_ref[...])
    k_new = _rope(kn_ref[...], c_ref[...], s1_ref[...], s2_ref[...])
    krot_ref[...] = k_new
    lanes = pl.ds(pl.multiple_of(g * 128, 128), 128)
    keys = jnp.concatenate([ck_ref[:, lanes], k_new], axis=0)
    vals = jnp.concatenate([cv_ref[:, lanes], vn_ref[...]], axis=0)
    mask = mask_ref[...] > 0.5
    _attend_groups(g, q_rot, keys, vals, mask, sink_ref, o_ref)


def _rope_tables(pos):
    half = ROT // 2
    f32 = np.float32
    inv_freq = f32(THETA) ** (-np.arange(half, dtype=f32) / f32(half))
    ang = pos.astype(f32)[:, None] * inv_freq[None, :]
    cos, sin = np.cos(ang).astype(f32), np.sin(ang).astype(f32)
    n = pos.shape[0]
    ones = np.ones((n, HD - ROT), f32)
    zeros = np.zeros((n, HD - ROT), f32)
    z8 = np.zeros((n, half), f32)
    c = np.concatenate([cos, cos, ones], axis=1)
    s1 = np.concatenate([z8, sin, zeros], axis=1)
    s2 = np.concatenate([-sin, z8, zeros], axis=1)
    return tuple(jnp.asarray(np.concatenate([t, t], axis=1)) for t in (c, s1, s2))


def _sample_mask():
    sb = SEQ_PER_STEP
    r = np.arange(sb * S_LEN)
    sq, qi = r // S_LEN, r % S_LEN
    c = np.arange(sb * WIN)
    m_cache = (c[None, :] // WIN == sq[:, None]) & (c[None, :] % WIN > qi[:, None])
    cn = np.arange(sb * S_LEN)
    m_new = (cn[None, :] // S_LEN == sq[:, None]) & (cn[None, :] % S_LEN <= qi[:, None])
    return np.concatenate([m_cache, m_new], axis=1).astype(np.float32)


def _attention(z32, z16, cache_k, cache_v, sinks):
    smem = pl.BlockSpec(memory_space=pltpu.SMEM)
    tabs_p = _rope_tables(np.arange(T_P))
    prev = lambda b: jnp.maximum(b - 1, 0)
    ck0, cv0 = Z32_K * CB // 128, Z32_V * CB // 128
    tab_cur = pl.BlockSpec((128, 128), lambda b, g: (b, 0))
    tab_prev = pl.BlockSpec((128, 128), lambda b, g: (prev(b), 0))
    y_p, krot_p = pl.pallas_call(
        _attn_prompt_kernel,
        grid=(T_P // 128, N_KV // 2),
        in_specs=[smem,
                  pl.BlockSpec((128, 2 * CB), lambda b, g: (b, g)),
                  pl.BlockSpec((128, 128), lambda b, g: (b, ck0 + g)),
                  pl.BlockSpec((128, 128), lambda b, g: (prev(b), ck0 + g)),
                  pl.BlockSpec((128, 128), lambda b, g: (b, cv0 + g)),
                  pl.BlockSpec((128, 128), lambda b, g: (prev(b), cv0 + g)),
                  tab_cur, tab_cur, tab_cur, tab_prev, tab_prev, tab_prev],
        out_specs=[pl.BlockSpec((128, 2 * CB), lambda b, g: (b, g)),
                   pl.BlockSpec((128, 128), lambda b, g: (b, g))],
        out_shape=[jax.ShapeDtypeStruct((T_P, D), BF16),
                   jax.ShapeDtypeStruct((T_P, KV_W), F32)],
        compiler_params=_cparams(2),
        name="attn_prompt",
    )(sinks, z16, z32, z32, z32, z32, *tabs_p, *tabs_p)

    sb = SEQ_PER_STEP
    rows = sb * S_LEN
    tabs_s = _rope_tables(PAST + (np.arange(rows) % S_LEN))
    mask = jnp.asarray(_sample_mask())
    nk = sb * WIN + rows
    rb0 = T_P // rows
    const = lambda shape: pl.BlockSpec(shape, lambda b, g: (0, 0))
    y_s, krot_s = pl.pallas_call(
        _attn_sample_kernel,
        grid=(N_SEQ // sb, N_KV // 2),
        in_specs=[smem,
                  pl.BlockSpec((rows, 2 * CB), lambda b, g: (b + rb0, g)),
                  pl.BlockSpec((rows, 128), lambda b, g: (b + rb0, ck0 + g)),
                  pl.BlockSpec((rows, 128), lambda b, g: (b + rb0, cv0 + g)),
                  pl.BlockSpec((sb * WIN, KV_W), lambda b, g: (b, 0)),
                  pl.BlockSpec((sb * WIN, KV_W), lambda b, g: (b, 0)),
                  const((rows, 128)), const((rows, 128)), const((rows, 128)),
                  const((rows, nk))],
        out_specs=[pl.BlockSpec((rows, 2 * CB), lambda b, g: (b, g)),
                   pl.BlockSpec((rows, 128), lambda b, g: (b, g))],
        out_shape=[jax.ShapeDtypeStruct((T_S, D), BF16),
                   jax.ShapeDtypeStruct((T_S, KV_W), F32)],
        compiler_params=_cparams(2),
        name="attn_sample",
    )(sinks, z16, z32, z32, cache_k.reshape(N_SEQ * WIN, KV_W), cache_v.reshape(N_SEQ * WIN, KV_W),
      *tabs_s, mask)
    return y_p, krot_p, y_s, krot_s


MT = 512


def _merge_kernel(yp_ref, ys_ref, xc_ref, bg_ref, cg_ref, ga_ref, gc_ref, xc8_ref, cg8_ref,
                  s1_ref, s2_ref, cw_ref, m_ref, cx_ref):
    i = pl.program_id(1)
    n_p = T_P // MT
    is_sample = i >= n_p
    y_att = jnp.where(is_sample, ys_ref[...], yp_ref[...]).astype(F32)
    cx = cg_ref[...] * xc_ref[...]
    cx_ref[...] = cx
    row = lax.broadcasted_iota(I32, cx.shape, 0)
    r1 = pltpu.roll(cx, 1, 0)
    r2 = pltpu.roll(cx, 2, 0)
    p8 = cg8_ref[...] * xc8_ref[...] * jnp.where(i > 0, 1.0, 0.0)
    pad = jnp.zeros((MT - 8, cx.shape[1]), F32)
    p1 = jnp.concatenate([pltpu.roll(p8, 1, 0), pad], axis=0)
    p2 = jnp.concatenate([pltpu.roll(p8, 2, 0), pad], axis=0)
    c1_p = jnp.where(row >= 1, r1, p1)
    c2_p = jnp.where(row >= 2, r2, p2)
    c1_s = jnp.where(row % S_LEN >= 1, r1, s1_ref[...])
    c2_s = jnp.where(row % S_LEN >= 2, r2, s2_ref[...])
    c1 = jnp.where(is_sample, c1_s, c1_p)
    c2 = jnp.where(is_sample, c2_s, c2_p)
    w = cw_ref[...]
    y_conv = bg_ref[...].astype(F32) * (w[0:1] * c2 + w[1:2] * c1 + w[2:3] * cx)
    merged = (jax.nn.sigmoid(ga_ref[...].astype(F32)) * y_att
              + jax.nn.sigmoid(gc_ref[...].astype(F32)) * y_conv)
    m_ref[...] = merged.astype(m_ref.dtype)


def _merge(z32, z16, y_p, y_s, state_conv, conv_w):
    n_p = T_P // MT
    zb = lambda off: pl.BlockSpec((MT, MT), lambda j, i: (i, off + j))
    z8 = lambda off: pl.BlockSpec((8, MT), lambda j, i: (jnp.maximum(i * (MT // 8) - 1, 0), off + j))
    cx_blk = pl.BlockSpec((MT, MT), lambda j, i: (jnp.maximum(i - (n_p - 1), 0), j))
    sblk = pl.BlockSpec((MT, MT), lambda j, i: (jnp.maximum(i - n_p, 0), j))
    st = state_conv
    s1 = jnp.concatenate([st[:, 1:2], jnp.zeros((N_SEQ, S_LEN - 1, D), F32)], axis=1).reshape(T_S, D)
    s2 = jnp.concatenate([st[:, 0:1], st[:, 1:2], jnp.zeros((N_SEQ, S_LEN - 2, D), F32)], axis=1).reshape(T_S, D)
    cw = jnp.concatenate([conv_w, jnp.zeros((5, D), F32)], axis=0)
    return pl.pallas_call(
        _merge_kernel,
        grid=(D // MT, T_ALL // MT),
        in_specs=[pl.BlockSpec((MT, MT), lambda j, i: (jnp.minimum(i, n_p - 1), j)),
                  sblk,
                  zb(Z32_XC), zb(Z16_B), zb(Z32_C), zb(Z16_GA), zb(Z16_GC), z8(Z32_XC), z8(Z32_C),
                  sblk, sblk,
                  pl.BlockSpec((8, MT), lambda j, i: (0, j))],
        out_specs=[pl.BlockSpec((MT, MT), lambda j, i: (i, j)), cx_blk],
        out_shape=[jax.ShapeDtypeStruct((T_ALL, D), BF16),
                   jax.ShapeDtypeStruct((MT + T_S, D), F32)],
        compiler_params=_cparams(2),
        name="merge",
    )(y_p, y_s, z32, z16, z32, z16, z16, z32, z32, s1, s2, cw)


RT = 256


def _pack_bf16_pair(lo, hi):
    lo_bits = pltpu.bitcast(lo.astype(BF16).astype(F32), U32)
    hi_bits = pltpu.bitcast(hi.astype(BF16).astype(F32), U32)
    return (lo_bits >> 16) | (hi_bits & jnp.uint32(0xFFFF0000))


def _unpack_bf16_pair(w):
    lo = pltpu.bitcast(w << 16, F32)
    hi = pltpu.bitcast(w & jnp.uint32(0xFFFF0000), F32)
    return lo, hi


def _router_kernel(h_ref, g_ref, wr_ref, br_ref, tri_ref, hn_ref, ti_ref, tg_ref, rk_ref, cnt_ref,
                   base_ref):
    @pl.when(pl.program_id(0) == 0)
    def _():
        base_ref[...] = jnp.zeros_like(base_ref)

    x = h_ref[...]
    ms = jnp.mean(x * x, axis=-1, keepdims=True)
    hn = x * lax.rsqrt(ms + EPS) * g_ref[...]
    hn_ref[...] = _pack_bf16_pair(hn[:, :D // 2], hn[:, D // 2:])
    w = wr_ref[...]
    h_hi = hn.astype(BF16)
    h_lo = (hn - h_hi.astype(F32)).astype(BF16)
    w_hi = w.astype(BF16)
    w_lo = (w - w_hi.astype(F32)).astype(BF16)
    logits = (jnp.dot(h_hi, w_hi, preferred_element_type=F32)
              + jnp.dot(h_hi, w_lo, preferred_element_type=F32)
              + jnp.dot(h_lo, w_hi, preferred_element_type=F32)) + br_ref[...]
    lane = lax.broadcasted_iota(I32, logits.shape, 1)
    vals = jnp.where(lane < N_E, logits, NEG)
    tops, idxs = [], []
    for _ in range(TOP_K):
        m = jnp.max(vals, axis=-1, keepdims=True)
        idx = jnp.min(jnp.where(vals == m, lane, 128), axis=-1, keepdims=True)
        tops.append(m)
        idxs.append(idx)
        vals = jnp.where(lane == idx, NEG, vals)
    es = [jnp.exp(t - tops[0]) for t in tops]
    den = es[0] + es[1] + es[2] + es[3]
    onehot = jnp.zeros(logits.shape, F32)
    for k in range(TOP_K):
        onehot = onehot + jnp.where(lane == idxs[k], 1.0, 0.0)
    before = jnp.dot(tri_ref[...], onehot.astype(BF16), preferred_element_type=F32) + base_ref[0:1, :]
    ti = jnp.zeros(logits.shape, I32)
    tg = jnp.zeros(logits.shape, F32)
    rk = jnp.zeros(logits.shape, F32)
    for k in range(TOP_K):
        ti = jnp.where(lane == k, idxs[k], ti)
        tg = jnp.where(lane == k, es[k] / den, tg)
        r = jnp.sum(jnp.where(lane == idxs[k], before, 0.0), axis=-1, keepdims=True)
        rk = jnp.where(lane == k, r, rk)
    ti_ref[...] = ti
    tg_ref[...] = tg
    rk_ref[...] = rk.astype(I32)
    total = base_ref[...] + jnp.sum(onehot, axis=0, keepdims=True)
    base_ref[...] = total
    cnt_ref[...] = total.astype(I32)


def _router(h1, g_ffn, w_router, b_router):
    wr = jnp.pad(w_router, ((0, 0), (0, 128 - N_E)))
    br = jnp.pad(b_router, (0, 128 - N_E)).reshape(1, 128)
    tri = jnp.asarray(np.tril(np.ones((RT, RT), np.float32), -1), BF16)
    blk128 = pl.BlockSpec((RT, 128), lambda i: (i, 0))
    return pl.pallas_call(
        _router_kernel,
        grid=(T_ALL // RT,),
        in_specs=[pl.BlockSpec((RT, D), lambda i: (i, 0)),
                  pl.BlockSpec((1, D), lambda i: (0, 0)),
                  pl.BlockSpec((D, 128), lambda i: (0, 0)),
                  pl.BlockSpec((1, 128), lambda i: (0, 0)),
                  pl.BlockSpec((RT, RT), lambda i: (0, 0))],
        out_specs=[pl.BlockSpec((RT, D // 2), lambda i: (i, 0)),
                   blk128, blk128, blk128,
                   pl.BlockSpec((8, 128), lambda i: (0, 0))],
        out_shape=[jax.ShapeDtypeStruct((T_ALL, D // 2), U32),
                   jax.ShapeDtypeStruct((T_ALL, 128), I32),
                   jax.ShapeDtypeStruct((T_ALL, 128), F32),
                   jax.ShapeDtypeStruct((T_ALL, 128), I32),
                   jax.ShapeDtypeStruct((8, 128), I32)],
        scratch_shapes=[pltpu.VMEM((8, 128), F32)],
        compiler_params=_cparams(1),
        name="router",
    )(h1, g_ffn.reshape(1, D), wr, br, tri)


def _row_copy(src_hbm, dst, sem, src_row, dst_row):
    return pltpu.make_async_copy(src_hbm.at[pl.ds(src_row, 1)], dst.at[pl.ds(dst_row, 1)], sem)


GU_TN = 512
GU_NJ = 2 * D // GU_TN
GU_ROWS = TM_E // GU_NJ


def _swiglu_pairs(gu):
    outs = []
    for q in range(2):
        a = gu[:, 256 * q:256 * q + 128]
        b = gu[:, 256 * q + 128:256 * q + 256]
        lane = lax.broadcasted_iota(I32, a.shape, 1)
        even = (lane & 1) == 0
        glu = jnp.where(even, a, pltpu.roll(b, 1, 1))
        lin = jnp.where(even, pltpu.roll(a, 127, 1), b)
        glu = jnp.minimum(glu, LIMIT)
        lin = jnp.clip(lin, -LIMIT, LIMIT)
        outs.append(glu * jax.nn.sigmoid(ALPHA * glu) * (lin + 1.0))
    return jnp.concatenate(outs, axis=1)


def _pair_perm():
    p = np.zeros((GU_TN // 2, GU_TN // 2), np.float32)
    for q in range(GU_TN // 256):
        for j in range(HD):
            for b in range(2):
                p[128 * q + 2 * j + b, 128 * q + HD * b + j] = 1.0
    return p


def _gu_kernel(e_ref, tok0_ref, tokn_ref, hn_hbm, w_ref, b_ref, p_ref, o_ref,
               stage_ref, lhs_ref, sem):
    k = pl.program_id(0)
    j = pl.program_id(1)

    def fetch(tok_ref, r0, n):
        for r in range(n):
            _row_copy(hn_hbm, stage_ref, sem, tok_ref[0, r0 + r], r0 + r).start(priority=1)

    def wait_block():
        pltpu.make_async_copy(hn_hbm.at[pl.ds(0, TM_E)], stage_ref, sem).wait()

    @pl.when(j == 0)
    def _():
        @pl.when(k == 0)
        def _():
            def body(c, carry):
                fetch(tok0_ref, c * GU_ROWS, GU_ROWS)
                return carry
            lax.fori_loop(0, GU_NJ, body, 0)

        wait_block()

        def body(c, carry):
            rows = pl.ds(pl.multiple_of(c * 128, 128), 128)
            lo, hi = _unpack_bf16_pair(stage_ref[rows, :])
            lhs_ref[rows, 0:D // 2] = lo.astype(BF16)
            lhs_ref[rows, D // 2:D] = hi.astype(BF16)
            return carry
        lax.fori_loop(0, TM_E // 128, body, 0)

    fetch(tokn_ref, j * GU_ROWS, GU_ROWS)
    gu = jnp.dot(lhs_ref[...], w_ref[...].astype(BF16), preferred_element_type=F32) + b_ref[...]
    act = _swiglu_pairs(gu).astype(BF16)
    o_ref[...] = jnp.dot(act, p_ref[...], preferred_element_type=F32).astype(BF16)

    @pl.when((k == pl.num_programs(0) - 1) & (j == GU_NJ - 1))
    def _():
        wait_block()


def _moe_gate_up(hn_packed, tok_of_slot, w, b, blk_e, n_used):
    tn_out = GU_TN // 2
    tok3 = tok_of_slot.reshape(NB_E, 1, TM_E)
    grid_spec = pltpu.PrefetchScalarGridSpec(
        num_scalar_prefetch=1,
        grid=(n_used, GU_NJ),
        in_specs=[pl.BlockSpec((None, 1, TM_E), lambda k, j, e: (0, 0, 0), memory_space=pltpu.SMEM),
                  pl.BlockSpec((None, 1, TM_E), lambda k, j, e: (jnp.minimum(k + 1, NB_E - 1), 0, 0),
                               memory_space=pltpu.SMEM),
                  pl.BlockSpec(memory_space=pl.ANY),
                  pl.BlockSpec((None, D, GU_TN), lambda k, j, e: (e[k], 0, j)),
                  pl.BlockSpec((None, 1, GU_TN), lambda k, j, e: (e[k], 0, j)),
                  pl.BlockSpec((tn_out, tn_out), lambda k, j, e: (0, 0))],
        out_specs=pl.BlockSpec((TM_E, tn_out), lambda k, j, e: (k, j)),
        scratch_shapes=[pltpu.VMEM((TM_E, D // 2), U32), pltpu.VMEM((TM_E, D), BF16),
                        pltpu.SemaphoreType.DMA(())],
    )
    return pl.pallas_call(
        _gu_kernel,
        grid_spec=grid_spec,
        out_shape=jax.ShapeDtypeStruct((CAP_E, GU_NJ * tn_out), BF16),
        compiler_params=_cparams(2),
        name="moe_gate_up",
    )(blk_e, tok3, tok3, hn_packed, w, b.reshape(N_E, 1, 2 * D), jnp.asarray(_pair_perm(), BF16))


DN_TN = 512


def _down_kernel(e_ref, a_ref, w_ref, b_ref, o_ref):
    y = jnp.dot(a_ref[...], w_ref[...].astype(BF16), preferred_element_type=F32) + b_ref[...]
    o_ref[...] = _pack_bf16_pair(y[:, :DN_TN // 2], y[:, DN_TN // 2:])


def _moe_down(act, w, b, blk_e, n_used):
    grid_spec = pltpu.PrefetchScalarGridSpec(
        num_scalar_prefetch=1,
        grid=(n_used, D // DN_TN),
        in_specs=[pl.BlockSpec((TM_E, D), lambda k, j, e: (k, 0)),
                  pl.BlockSpec((None, D, DN_TN), lambda k, j, e: (e[k], 0, j)),
                  pl.BlockSpec((None, 1, DN_TN), lambda k, j, e: (e[k], 0, j))],
        out_specs=pl.BlockSpec((TM_E, DN_TN // 2), lambda k, j, e: (k, j)),
    )
    return pl.pallas_call(
        _down_kernel,
        grid_spec=grid_spec,
        out_shape=jax.ShapeDtypeStruct((CAP_E, D // 2), U32),
        compiler_params=_cparams(2),
        name="moe_down",
    )(blk_e, act, w, b.reshape(N_E, 1, D))


def _combine_kernel(slot_ref, slotn_ref, g_ref, h_ref, gp_ref, y_hbm, h2_ref, hn_ref, buf_ref, sems):
    i = pl.program_id(0)
    n = pl.num_programs(0)
    cur = i % 2

    def fetch(idx_ref, slot):
        def body(c, carry):
            for r in range(8):
                row = c * 8 + r
                for k in range(TOP_K):
                    _row_copy(y_hbm, buf_ref.at[slot, k], sems.at[slot],
                              idx_ref[0, TOP_K * row + k], row).start(priority=k % 2)
            return carry
        lax.fori_loop(0, GT // 8, body, 0)

    @pl.when(i == 0)
    def _():
        fetch(slot_ref, 0)

    @pl.when(i + 1 < n)
    def _():
        fetch(slotn_ref, 1 - cur)

    for k in range(TOP_K):
        pltpu.make_async_copy(y_hbm.at[pl.ds(0, GT)], buf_ref.at[cur, k], sems.at[cur]).wait()

    g = g_ref[...]
    tw = DN_TN // 2
    for c in range(D // DN_TN):
        lo_acc = h_ref[:, DN_TN * c:DN_TN * c + tw]
        hi_acc = h_ref[:, DN_TN * c + tw:DN_TN * (c + 1)]
        for k in range(TOP_K):
            lo, hi = _unpack_bf16_pair(buf_ref[cur, k, :, tw * c:tw * (c + 1)])
            lo_acc = lo_acc + g[:, k:k + 1] * lo
            hi_acc = hi_acc + g[:, k:k + 1] * hi
        h2_ref[:, DN_TN * c:DN_TN * c + tw] = lo_acc
        h2_ref[:, DN_TN * c + tw:DN_TN * (c + 1)] = hi_acc
    h2 = h2_ref[...]
    ms = jnp.mean(h2 * h2, axis=-1, keepdims=True)
    hn_ref[...] = (h2 * lax.rsqrt(ms + EPS) * gp_ref[...]).astype(BF16)


def _moe_combine(slot, gates, h1, g_ple, y_packed):
    n = T_ALL // GT
    slot3 = slot.reshape(n, 1, GT * TOP_K)
    return pl.pallas_call(
        _combine_kernel,
        grid=(n,),
        in_specs=[pl.BlockSpec((None, 1, GT * TOP_K), lambda i: (i, 0, 0), memory_space=pltpu.SMEM),
                  pl.BlockSpec((None, 1, GT * TOP_K), lambda i: (jnp.minimum(i + 1, n - 1), 0, 0),
                               memory_space=pltpu.SMEM),
                  pl.BlockSpec((GT, 128), lambda i: (i, 0)),
                  pl.BlockSpec((GT, D), lambda i: (i, 0)),
                  pl.BlockSpec((1, D), lambda i: (0, 0)),
                  pl.BlockSpec(memory_space=pl.ANY)],
        out_specs=[pl.BlockSpec((GT, D), lambda i: (i, 0)),
                   pl.BlockSpec((GT, D), lambda i: (i, 0))],
        out_shape=[jax.ShapeDtypeStruct((T_ALL, D), F32),
                   jax.ShapeDtypeStruct((T_ALL, D), BF16)],
        scratch_shapes=[pltpu.VMEM((2, TOP_K, GT, D // 2), U32), pltpu.SemaphoreType.DMA((2,))],
        compiler_params=_cparams(1),
        name="moe_combine",
    )(slot3, slot3, gates, h1, g_ple.reshape(1, D), y_packed)


def _routing_tables(top_i, rank, counts):
    flat_e = top_i.reshape(-1)
    nblk = (counts + TM_E - 1) // TM_E
    blk_end = jnp.cumsum(nblk)
    blk_start = blk_end - nblk
    slot = (blk_start[flat_e] * TM_E + rank.reshape(-1)).astype(I32)
    n_used = blk_end[-1].astype(I32)
    ks = jnp.arange(NB_E, dtype=I32)
    blk_e = jnp.minimum(jnp.searchsorted(blk_end, ks, side="right"), N_E - 1).astype(I32)
    flat_t = jnp.repeat(jnp.arange(T_ALL, dtype=I32), TOP_K)
    tok_of_slot = jnp.zeros((CAP_E,), I32).at[slot].set(flat_t, unique_indices=True)
    return slot, tok_of_slot, blk_e, n_used


def kernel(x_prompt, x_sample, cache_k, cache_v, state_conv, p_prompt, p_sample, g_mix, w_in, b_in, conv_w, sinks, w_out, b_out, g_ffn, w_router, b_router, w_gate_up, b_gate_up, w_down, b_down, g_ple, w_ple_gate, w_ple, g_final):
    xp = x_prompt[0]
    xs = x_sample.reshape(T_S, D)
    p_all = jnp.concatenate([p_prompt[0, 0], p_sample[0].reshape(T_S, PLE)], axis=0)

    u = _rms_norm_stacked(xp, xs, g_mix[0])
    b_in2 = b_in[0].reshape(1, N_IN)
    z32 = _dense(_mm_bias_kernel, u, w_in[0], [b_in2],
                 [pl.BlockSpec((1, CB), lambda i, j: (0, _z32_col(j)))], 1024, CB, "in_proj_f32",
                 col_map=_z32_col, n_tiles=Z32_TILES)
    z16 = _dense(_mm_bias_kernel, u, w_in[0], [b_in2],
                 [pl.BlockSpec((1, CB), lambda i, j: (0, _z16_col(j)))], 1024, CB, "in_proj_bf16",
                 col_map=_z16_col, n_tiles=Z16_TILES, out_dtype=BF16)

    y_p, krot_p, y_s, krot_s = _attention(z32, z16, cache_k[0], cache_v[0], sinks[0])

    merged, cx = _merge(z32, z16, y_p, y_s, state_conv[0], conv_w[0])
    n_p = T_P // 1024
    h1 = _dense(functools.partial(_mm_bias_res_kernel, n_p=n_p), merged, w_out[0],
                [b_out[0].reshape(1, D), xp, xs],
                [pl.BlockSpec((1, 512), lambda i, j: (0, j)),
                 pl.BlockSpec((1024, 512), lambda i, j: (jnp.minimum(i, n_p - 1), j)),
                 pl.BlockSpec((1024, 512), lambda i, j: (jnp.maximum(i - n_p, 0), j))],
                1024, 512, "out_proj")

    hn_packed, top_i, top_g, rank, counts = _router(h1, g_ffn[0], w_router[0], b_router[0])
    slot, tok_of_slot, blk_e, n_used = _routing_tables(
        top_i[:, :TOP_K], rank[:, :TOP_K], counts[0, :N_E])
    act = _moe_gate_up(hn_packed, tok_of_slot, w_gate_up[0], b_gate_up[0], blk_e, n_used)
    y_packed = _moe_down(act, w_down[0], b_down[0], blk_e, n_used)
    h2, hn2 = _moe_combine(slot, top_g, h1, g_ple[0], y_packed)

    h3 = _dense(_mm_ple_kernel, hn2, w_ple_gate[0], [h2, p_all, w_ple[0]],
                [pl.BlockSpec((1024, 512), lambda i, j: (i, j)),
                 pl.BlockSpec((1024, PLE), lambda i, j: (i, 0)),
                 pl.BlockSpec((PLE, 512), lambda i, j: (0, j))], 1024, 512, "ple")

    y_prompt = _rms_norm(h3, g_final, F32, 0, T_P).reshape(1, T_P, D)
    y_sample = _rms_norm(h3, g_final, F32, T_P, T_S).reshape(N_SEQ, S_LEN, D)

    w_p = min(WIN, T_P)
    k_win_p = krot_p[T_P - w_p:].reshape(1, 1, w_p, N_KV, HD)
    v_win_p = z32[T_P - w_p:T_P, Z32_V * CB:Z32_V * CB + KV_W].reshape(1, 1, w_p, N_KV, HD)
    conv_p = cx[MT - 2:MT].reshape(1, 1, 2, D)
    k_new = krot_s.reshape(N_SEQ, S_LEN, N_KV, HD)
    v_new = z32[T_P:, Z32_V * CB:Z32_V * CB + KV_W].reshape(N_SEQ, S_LEN, N_KV, HD)
    k_win_s = jnp.concatenate([cache_k[0][:, S_LEN:], k_new], axis=1)[None]
    v_win_s = jnp.concatenate([cache_v[0][:, S_LEN:], v_new], axis=1)[None]
    conv_s = cx[MT:].reshape(N_SEQ, S_LEN, D)[:, S_LEN - 2:][None]
    return (y_prompt, y_sample, k_win_p, v_win_p, conv_p, k_win_s, v_win_s, conv_s)
```

```python
import functools

import jax
import jax.numpy as jnp
import numpy as np
from jax import lax
from jax.experimental import pallas as pl
from jax.experimental.pallas import tpu as pltpu

BF16 = jnp.bfloat16
F32 = jnp.float32
I32 = jnp.int32
U32 = jnp.uint32

D = 4096
T_P = 8192
N_SEQ = 128
S_LEN = 8
T_S = N_SEQ * S_LEN
T_ALL = T_P + T_S
PAST = 8192
HD = 64
N_KV = 8
KV_W = N_KV * HD
ROT = 16
THETA = 500000.0
WIN = 128
N_E = 32
TOP_K = 4
N_ASSIGN = T_ALL * TOP_K
LIMIT = 7.0
ALPHA = 1.702
PLE = 256
EPS = 1e-5
NEG = -1e30
N_IN = 25600

CB = 512
W_Q, W_K, W_V, W_XC, W_B, W_C, W_GA, W_GC = 0, 8, 9, 10, 18, 26, 34, 42
Z32_K, Z32_V, Z32_XC, Z32_C, Z32_TILES = 0, 1, 2, 10, 18
Z16_Q, Z16_B, Z16_GA, Z16_GC, Z16_TILES = 0, 8, 16, 24, 32


def _z32_col(j):
    return jnp.where(j < Z32_C, W_K + j, W_C + (j - Z32_C))


def _z16_col(j):
    return jnp.where(j < Z16_B, W_Q + j, jnp.where(j < Z16_GA, W_B + (j - Z16_B), W_GA + (j - Z16_GA)))

VMEM_LIMIT = 56 * 1024 * 1024

TM_E = 1280
NB_E = N_ASSIGN // TM_E + N_E + 1
CAP_E = NB_E * TM_E
GT = 256
SEQ_PER_STEP = 8


def _cparams(n_axes):
    return pltpu.CompilerParams(dimension_semantics=("arbitrary",) * n_axes,
                                vmem_limit_bytes=VMEM_LIMIT)


def _rms_cast_kernel(x_ref, g_ref, o_ref):
    x = x_ref[...]
    ms = jnp.mean(x * x, axis=-1, keepdims=True)
    o_ref[...] = (x * lax.rsqrt(ms + EPS) * g_ref[...]).astype(o_ref.dtype)


def _rms_norm(x, g, out_dtype, row0=0, rows=None, tm=512):
    rows = x.shape[0] if rows is None else rows
    b0 = row0 // tm
    return pl.pallas_call(
        _rms_cast_kernel,
        grid=(rows // tm,),
        in_specs=[pl.BlockSpec((tm, D), lambda i: (i + b0, 0)),
                  pl.BlockSpec((1, D), lambda i: (0, 0))],
        out_specs=pl.BlockSpec((tm, D), lambda i: (i, 0)),
        out_shape=jax.ShapeDtypeStruct((rows, D), out_dtype),
        compiler_params=_cparams(1),
        name="rms_norm",
    )(x, g.reshape(1, D))


def _rms_stacked_kernel(xp_ref, xs_ref, g_ref, o_ref, *, n_p):
    x = jnp.where(pl.program_id(0) < n_p, xp_ref[...], xs_ref[...])
    ms = jnp.mean(x * x, axis=-1, keepdims=True)
    o_ref[...] = (x * lax.rsqrt(ms + EPS) * g_ref[...]).astype(o_ref.dtype)


def _rms_norm_stacked(xp, xs, g, tm=512):
    n_p = T_P // tm
    return pl.pallas_call(
        functools.partial(_rms_stacked_kernel, n_p=n_p),
        grid=(T_ALL // tm,),
        in_specs=[pl.BlockSpec((tm, D), lambda i: (jnp.minimum(i, n_p - 1), 0)),
                  pl.BlockSpec((tm, D), lambda i: (jnp.maximum(i - n_p, 0), 0)),
                  pl.BlockSpec((1, D), lambda i: (0, 0))],
        out_specs=pl.BlockSpec((tm, D), lambda i: (i, 0)),
        out_shape=jax.ShapeDtypeStruct((T_ALL, D), BF16),
        compiler_params=_cparams(1),
        name="rms_norm_in",
    )(xp, xs, g.reshape(1, D))


def _mm_bias_kernel(lhs_ref, w_ref, b_ref, o_ref):
    acc = jnp.dot(lhs_ref[...], w_ref[...].astype(BF16), preferred_element_type=F32)
    o_ref[...] = (acc + b_ref[...]).astype(o_ref.dtype)


def _mm_bias_res_kernel(lhs_ref, w_ref, b_ref, rp_ref, rs_ref, o_ref, *, n_p):
    acc = jnp.dot(lhs_ref[...], w_ref[...].astype(BF16), preferred_element_type=F32)
    res = jnp.where(pl.program_id(0) < n_p, rp_ref[...], rs_ref[...])
    o_ref[...] = acc + b_ref[...] + res


def _mm_ple_kernel(lhs_ref, w_ref, h_ref, p_ref, wp_ref, o_ref):
    acc = jnp.dot(lhs_ref[...], w_ref[...].astype(BF16), preferred_element_type=F32)
    pp = jnp.dot(p_ref[...].astype(BF16), wp_ref[...].astype(BF16), preferred_element_type=F32)
    o_ref[...] = h_ref[...] + jax.nn.sigmoid(acc) * pp


def _dense(kernel, lhs, w, extras, extra_specs, tm, tn, name, col_map=None, n_tiles=None,
           out_dtype=F32):
    M, K = lhs.shape
    n_tiles = w.shape[1] // tn if n_tiles is None else n_tiles
    col_map = (lambda j: j) if col_map is None else col_map
    return pl.pallas_call(
        kernel,
        grid=(M // tm, n_tiles),
        in_specs=[pl.BlockSpec((tm, K), lambda i, j: (i, 0)),
                  pl.BlockSpec((K, tn), lambda i, j: (0, col_map(j)))] + extra_specs,
        out_specs=pl.BlockSpec((tm, tn), lambda i, j: (i, j)),
        out_shape=jax.ShapeDtypeStruct((M, n_tiles * tn), out_dtype),
        compiler_params=_cparams(2),
        name=name,
    )(lhs, w, *extras)


def _rope(x, c, s1, s2):
    w = x.shape[1]
    reps = w // 128
    ct = jnp.concatenate([c] * reps, axis=1) if reps > 1 else c
    s1t = jnp.concatenate([s1] * reps, axis=1) if reps > 1 else s1
    s2t = jnp.concatenate([s2] * reps, axis=1) if reps > 1 else s2
    return x * ct + pltpu.roll(x, 8, 1) * s1t + pltpu.roll(x, w - 8, 1) * s2t


def _half_placed(chunk):
    lane = lax.broadcasted_iota(I32, chunk.shape, 1)
    low = lane < HD
    other = pltpu.roll(chunk, HD, 1)
    zero = jnp.zeros_like(chunk)
    a = (jnp.where(low, chunk, zero).astype(BF16), jnp.where(low, zero, other).astype(BF16))
    b = (jnp.where(low, other, zero).astype(BF16), jnp.where(low, zero, chunk).astype(BF16))
    return a, b


def _attend_groups(c, q_rot, keys, vals, mask, sink_ref, o_ref):
    k_ab = _half_placed(keys)
    v_ab = _half_placed(vals)
    n_rows = q_rot.shape[0]
    mask4 = jnp.concatenate([mask] * 4, axis=0)
    for grp in range(2):
        q4 = jnp.concatenate([q_rot[:, 512 * grp + 128 * i:512 * grp + 128 * (i + 1)] for i in range(4)],
                             axis=0)
        q4 = (q4 * (HD ** -0.5)).astype(BF16)
        out = None
        for par in range(2):
            k2, v2 = k_ab[grp][par], v_ab[grp][par]
            sink = jnp.concatenate(
                [jnp.full((n_rows, 1), sink_ref[16 * c + 8 * grp + 2 * i + par], F32) for i in range(4)],
                axis=0)
            s = lax.dot_general(q4, k2, (((1,), (1,)), ((), ())), preferred_element_type=F32)
            s = jnp.where(mask4, s, NEG)
            m = jnp.maximum(jnp.max(s, axis=-1, keepdims=True), sink)
            p = jnp.exp(s - m)
            den = jnp.sum(p, axis=-1, keepdims=True) + jnp.exp(sink - m)
            o = jnp.dot(p.astype(BF16), v2, preferred_element_type=F32) / den
            out = o if out is None else out + o
        for i in range(4):
            col = 512 * grp + 128 * i
            o_ref[:, col:col + 128] = out[n_rows * i:n_rows * (i + 1)].astype(o_ref.dtype)


def _attn_prompt_kernel(sink_ref, q_ref, kc_ref, kp_ref, vc_ref, vp_ref,
                        cc_ref, s1c_ref, s2c_ref, cp_ref, s1p_ref, s2p_ref,
                        o_ref, krot_ref):
    b = pl.program_id(0)
    g = pl.program_id(1)
    q_rot = _rope(q_ref[...].astype(F32), cc_ref[...], s1c_ref[...], s2c_ref[...])
    k_cur = _rope(kc_ref[...], cc_ref[...], s1c_ref[...], s2c_ref[...])
    k_prev = _rope(kp_ref[...], cp_ref[...], s1p_ref[...], s2p_ref[...])
    krot_ref[...] = k_cur
    keys = jnp.concatenate([k_prev, k_cur], axis=0)
    vals = jnp.concatenate([vp_ref[...], vc_ref[...]], axis=0)
    r = lax.broadcasted_iota(I32, (128, 256), 0)
    c = lax.broadcasted_iota(I32, (128, 256), 1)
    mask = (c > r) & (c <= r + WIN) & ((b > 0) | (c >= 128))
    _attend_groups(g, q_rot, keys, vals, mask, sink_ref, o_ref)


def _attn_sample_kernel(sink_ref, q_ref, kn_ref, vn_ref, ck_ref, cv_ref,
                        c_ref, s1_ref, s2_ref, mask_ref, o_ref, krot_ref):
    g = pl.program_id(1)
    q_rot = _rope(q_ref[...].astype(F32), c_ref[...], s1_ref[...], s2_ref[...])
    k_new = _rope(kn_ref[...], c_ref[...], s1_ref[...], s2_ref[...])
    krot_ref[...] = k_new
    lanes = pl.ds(pl.multiple_of(g * 128, 128), 128)
    keys = jnp.concatenate([ck_ref[:, lanes], k_new], axis=0)
    vals = jnp.concatenate([cv_ref[:, lanes], vn_ref[...]], axis=0)
    mask = mask_ref[...] > 0.5
    _attend_groups(g, q_rot, keys, vals, mask, sink_ref, o_ref)


def _rope_tables(pos):
    half = ROT // 2
    f32 = np.float32
    inv_freq = f32(THETA) ** (-np.arange(half, dtype=f32) / f32(half))
    ang = pos.astype(f32)[:, None] * inv_freq[None, :]
    cos, sin = np.cos(ang).astype(f32), np.sin(ang).astype(f32)
    n = pos.shape[0]
    ones = np.ones((n, HD - ROT), f32)
    zeros = np.zeros((n, HD - ROT), f32)
    z8 = np.zeros((n, half), f32)
    c = np.concatenate([cos, cos, ones], axis=1)
    s1 = np.concatenate([z8, sin, zeros], axis=1)
    s2 = np.concatenate([-sin, z8, zeros], axis=1)
    return tuple(jnp.asarray(np.concatenate([t, t], axis=1)) for t in (c, s1, s2))


def _sample_mask():
    sb = SEQ_PER_STEP
    r = np.arange(sb * S_LEN)
    sq, qi = r // S_LEN, r % S_LEN
    c = np.arange(sb * WIN)
    m_cache = (c[None, :] // WIN == sq[:, None]) & (c[None, :] % WIN > qi[:, None])
    cn = np.arange(sb * S_LEN)
    m_new = (cn[None, :] // S_LEN == sq[:, None]) & (cn[None, :] % S_LEN <= qi[:, None])
    return np.concatenate([m_cache, m_new], axis=1).astype(np.float32)


def _attention(z32, z16, cache_k, cache_v, sinks):
    smem = pl.BlockSpec(memory_space=pltpu.SMEM)
    tabs_p = _rope_tables(np.arange(T_P))
    prev = lambda b: jnp.maximum(b - 1, 0)
    ck0, cv0 = Z32_K * CB // 128, Z32_V * CB // 128
    tab_cur = pl.BlockSpec((128, 128), lambda b, g: (b, 0))
    tab_prev = pl.BlockSpec((128, 128), lambda b, g: (prev(b), 0))
    y_p, krot_p = pl.pallas_call(
        _attn_prompt_kernel,
        grid=(T_P // 128, N_KV // 2),
        in_specs=[smem,
                  pl.BlockSpec((128, 2 * CB), lambda b, g: (b, g)),
                  pl.BlockSpec((128, 128), lambda b, g: (b, ck0 + g)),
                  pl.BlockSpec((128, 128), lambda b, g: (prev(b), ck0 + g)),
                  pl.BlockSpec((128, 128), lambda b, g: (b, cv0 + g)),
                  pl.BlockSpec((128, 128), lambda b, g: (prev(b), cv0 + g)),
                  tab_cur, tab_cur, tab_cur, tab_prev, tab_prev, tab_prev],
        out_specs=[pl.BlockSpec((128, 2 * CB), lambda b, g: (b, g)),
                   pl.BlockSpec((128, 128), lambda b, g: (b, g))],
        out_shape=[jax.ShapeDtypeStruct((T_P, D), BF16),
                   jax.ShapeDtypeStruct((T_P, KV_W), F32)],
        compiler_params=_cparams(2),
        name="attn_prompt",
    )(sinks, z16, z32, z32, z32, z32, *tabs_p, *tabs_p)

    sb = SEQ_PER_STEP
    rows = sb * S_LEN
    tabs_s = _rope_tables(PAST + (np.arange(rows) % S_LEN))
    mask = jnp.asarray(_sample_mask())
    nk = sb * WIN + rows
    rb0 = T_P // rows
    const = lambda shape: pl.BlockSpec(shape, lambda b, g: (0, 0))
    y_s, krot_s = pl.pallas_call(
        _attn_sample_kernel,
        grid=(N_SEQ // sb, N_KV // 2),
        in_specs=[smem,
                  pl.BlockSpec((rows, 2 * CB), lambda b, g: (b + rb0, g)),
                  pl.BlockSpec((rows, 128), lambda b, g: (b + rb0, ck0 + g)),
                  pl.BlockSpec((rows, 128), lambda b, g: (b + rb0, cv0 + g)),
                  pl.BlockSpec((sb * WIN, KV_W), lambda b, g: (b, 0)),
                  pl.BlockSpec((sb * WIN, KV_W), lambda b, g: (b, 0)),
                  const((rows, 128)), const((rows, 128)), const((rows, 128)),
                  const((rows, nk))],
        out_specs=[pl.BlockSpec((rows, 2 * CB), lambda b, g: (b, g)),
                   pl.BlockSpec((rows, 128), lambda b, g: (b, g))],
        out_shape=[jax.ShapeDtypeStruct((T_S, D), BF16),
                   jax.ShapeDtypeStruct((T_S, KV_W), F32)],
        compiler_params=_cparams(2),
        name="attn_sample",
    )(sinks, z16, z32, z32, cache_k.reshape(N_SEQ * WIN, KV_W), cache_v.reshape(N_SEQ * WIN, KV_W),
      *tabs_s, mask)
    return y_p, krot_p, y_s, krot_s


MT = 512


def _merge_kernel(yp_ref, ys_ref, xc_ref, bg_ref, cg_ref, ga_ref, gc_ref, xc8_ref, cg8_ref,
                  s1_ref, s2_ref, cw_ref, m_ref, cx_ref):
    i = pl.program_id(1)
    n_p = T_P // MT
    is_sample = i >= n_p
    y_att = jnp.where(is_sample, ys_ref[...], yp_ref[...]).astype(F32)
    cx = cg_ref[...] * xc_ref[...]
    cx_ref[...] = cx
    row = lax.broadcasted_iota(I32, cx.shape, 0)
    r1 = pltpu.roll(cx, 1, 0)
    r2 = pltpu.roll(cx, 2, 0)
    p8 = cg8_ref[...] * xc8_ref[...] * jnp.where(i > 0, 1.0, 0.0)
    pad = jnp.zeros((MT - 8, cx.shape[1]), F32)
    p1 = jnp.concatenate([pltpu.roll(p8, 1, 0), pad], axis=0)
    p2 = jnp.concatenate([pltpu.roll(p8, 2, 0), pad], axis=0)
    c1_p = jnp.where(row >= 1, r1, p1)
    c2_p = jnp.where(row >= 2, r2, p2)
    c1_s = jnp.where(row % S_LEN >= 1, r1, s1_ref[...])
    c2_s = jnp.where(row % S_LEN >= 2, r2, s2_ref[...])
    c1 = jnp.where(is_sample, c1_s, c1_p)
    c2 = jnp.where(is_sample, c2_s, c2_p)
    w = cw_ref[...]
    y_conv = bg_ref[...].astype(F32) * (w[0:1] * c2 + w[1:2] * c1 + w[2:3] * cx)
    merged = (jax.nn.sigmoid(ga_ref[...].astype(F32)) * y_att
              + jax.nn.sigmoid(gc_ref[...].astype(F32)) * y_conv)
    m_ref[...] = merged.astype(m_ref.dtype)


def _merge(z32, z16, y_p, y_s, state_conv, conv_w):
    n_p = T_P // MT
    zb = lambda off: pl.BlockSpec((MT, MT), lambda j, i: (i, off + j))
    z8 = lambda off: pl.BlockSpec((8, MT), lambda j, i: (jnp.maximum(i * (MT // 8) - 1, 0), off + j))
    cx_blk = pl.BlockSpec((MT, MT), lambda j, i: (jnp.maximum(i - (n_p - 1), 0), j))
    sblk = pl.BlockSpec((MT, MT), lambda j, i: (jnp.maximum(i - n_p, 0), j))
    st = state_conv
    s1 = jnp.concatenate([st[:, 1:2], jnp.zeros((N_SEQ, S_LEN - 1, D), F32)], axis=1).reshape(T_S, D)
    s2 = jnp.concatenate([st[:, 0:1], st[:, 1:2], jnp.zeros((N_SEQ, S_LEN - 2, D), F32)], axis=1).reshape(T_S, D)
    cw = jnp.concatenate([conv_w, jnp.zeros((5, D), F32)], axis=0)
    return pl.pallas_call(
        _merge_kernel,
        grid=(D // MT, T_ALL // MT),
        in_specs=[pl.BlockSpec((MT, MT), lambda j, i: (jnp.minimum(i, n_p - 1), j)),
                  sblk,
                  zb(Z32_XC), zb(Z16_B), zb(Z32_C), zb(Z16_GA), zb(Z16_GC), z8(Z32_XC), z8(Z32_C),
                  sblk, sblk,
                  pl.BlockSpec((8, MT), lambda j, i: (0, j))],
        out_specs=[pl.BlockSpec((MT, MT), lambda j, i: (i, j)), cx_blk],
        out_shape=[jax.ShapeDtypeStruct((T_ALL, D), BF16),
                   jax.ShapeDtypeStruct((MT + T_S, D), F32)],
        compiler_params=_cparams(2),
        name="merge",
    )(y_p, y_s, z32, z16, z32, z16, z16, z32, z32, s1, s2, cw)


RT = 512


def _pack_bf16_pair(lo, hi):
    lo_bits = pltpu.bitcast(lo.astype(BF16).astype(F32), U32)
    hi_bits = pltpu.bitcast(hi.astype(BF16).astype(F32), U32)
    return (lo_bits >> 16) | (hi_bits & jnp.uint32(0xFFFF0000))


def _unpack_bf16_pair(w):
    lo = pltpu.bitcast(w << 16, F32)
    hi = pltpu.bitcast(w & jnp.uint32(0xFFFF0000), F32)
    return lo, hi


def _router_kernel(h_ref, g_ref, wr_ref, br_ref, tri_ref, hn_ref, ti_ref, tg_ref, rk_ref, cnt_ref,
                   base_ref):
    @pl.when(pl.program_id(0) == 0)
    def _():
        base_ref[...] = jnp.zeros_like(base_ref)

    x = h_ref[...]
    ms = jnp.mean(x * x, axis=-1, keepdims=True)
    hn = x * lax.rsqrt(ms + EPS) * g_ref[...]
    hn_ref[...] = _pack_bf16_pair(hn[:, :D // 2], hn[:, D // 2:])
    w = wr_ref[...]
    h_hi = hn.astype(BF16)
    h_lo = (hn - h_hi.astype(F32)).astype(BF16)
    w_hi = w.astype(BF16)
    w_lo = (w - w_hi.astype(F32)).astype(BF16)
    logits = (jnp.dot(h_hi, w_hi, preferred_element_type=F32)
              + jnp.dot(h_hi, w_lo, preferred_element_type=F32)
              + jnp.dot(h_lo, w_hi, preferred_element_type=F32)) + br_ref[...]
    lane = lax.broadcasted_iota(I32, logits.shape, 1)
    vals = jnp.where(lane < N_E, logits, NEG)
    tops, idxs = [], []
    for _ in range(TOP_K):
        m = jnp.max(vals, axis=-1, keepdims=True)
        idx = jnp.min(jnp.where(vals == m, lane, 128), axis=-1, keepdims=True)
        tops.append(m)
        idxs.append(idx)
        vals = jnp.where(lane == idx, NEG, vals)
    es = [jnp.exp(t - tops[0]) for t in tops]
    den = es[0] + es[1] + es[2] + es[3]
    onehot = jnp.zeros(logits.shape, F32)
    for k in range(TOP_K):
        onehot = onehot + jnp.where(lane == idxs[k], 1.0, 0.0)
    before = jnp.dot(tri_ref[...], onehot.astype(BF16), preferred_element_type=F32) + base_ref[0:1, :]
    ti = jnp.zeros(logits.shape, I32)
    tg = jnp.zeros(logits.shape, F32)
    rk = jnp.zeros(logits.shape, F32)
    for k in range(TOP_K):
        ti = jnp.where(lane == k, idxs[k], ti)
        tg = jnp.where(lane == k, es[k] / den, tg)
        r = jnp.sum(jnp.where(lane == idxs[k], before, 0.0), axis=-1, keepdims=True)
        rk = jnp.where(lane == k, r, rk)
    ti_ref[...] = ti
    tg_ref[...] = tg
    rk_ref[...] = rk.astype(I32)
    total = base_ref[...] + jnp.sum(onehot, axis=0, keepdims=True)
    base_ref[...] = total
    cnt_ref[...] = total.astype(I32)


def _router(h1, g_ffn, w_router, b_router):
    wr = jnp.pad(w_router, ((0, 0), (0, 128 - N_E)))
    br = jnp.pad(b_router, (0, 128 - N_E)).reshape(1, 128)
    tri = jnp.asarray(np.tril(np.ones((RT, RT), np.float32), -1), BF16)
    blk128 = pl.BlockSpec((RT, 128), lambda i: (i, 0))
    return pl.pallas_call(
        _router_kernel,
        grid=(T_ALL // RT,),
        in_specs=[pl.BlockSpec((RT, D), lambda i: (i, 0)),
                  pl.BlockSpec((1, D), lambda i: (0, 0)),
                  pl.BlockSpec((D, 128), lambda i: (0, 0)),
                  pl.BlockSpec((1, 128), lambda i: (0, 0)),
                  pl.BlockSpec((RT, RT), lambda i: (0, 0))],
        out_specs=[pl.BlockSpec((RT, D // 2), lambda i: (i, 0)),
                   blk128, blk128, blk128,
                   pl.BlockSpec((8, 128), lambda i: (0, 0))],
        out_shape=[jax.ShapeDtypeStruct((T_ALL, D // 2), U32),
                   jax.ShapeDtypeStruct((T_ALL, 128), I32),
                   jax.ShapeDtypeStruct((T_ALL, 128), F32),
                   jax.ShapeDtypeStruct((T_ALL, 128), I32),
                   jax.ShapeDtypeStruct((8, 128), I32)],
        scratch_shapes=[pltpu.VMEM((8, 128), F32)],
        compiler_params=_cparams(1),
        name="router",
    )(h1, g_ffn.reshape(1, D), wr, br, tri)


def _row_copy(src_hbm, dst, sem, src_row, dst_row):
    return pltpu.make_async_copy(src_hbm.at[pl.ds(src_row, 1)], dst.at[pl.ds(dst_row, 1)], sem)


GU_TN = 512
GU_NJ = 2 * D // GU_TN
GU_ROWS = TM_E // GU_NJ


def _swiglu_pairs(gu):
    outs = []
    for q in range(2):
        a = gu[:, 256 * q:256 * q + 128]
        b = gu[:, 256 * q + 128:256 * q + 256]
        lane = lax.broadcasted_iota(I32, a.shape, 1)
        even = (lane & 1) == 0
        glu = jnp.where(even, a, pltpu.roll(b, 1, 1))
        lin = jnp.where(even, pltpu.roll(a, 127, 1), b)
        glu = jnp.minimum(glu, LIMIT)
        lin = jnp.clip(lin, -LIMIT, LIMIT)
        outs.append(glu * jax.nn.sigmoid(ALPHA * glu) * (lin + 1.0))
    return jnp.concatenate(outs, axis=1)


def _pair_perm():
    p = np.zeros((GU_TN // 2, GU_TN // 2), np.float32)
    for q in range(GU_TN // 256):
        for j in range(HD):
            for b in range(2):
                p[128 * q + 2 * j + b, 128 * q + HD * b + j] = 1.0
    return p


def _gu_kernel(e_ref, tok0_ref, tokn_ref, hn_hbm, w_ref, b_ref, p_ref, o_ref,
               stage_ref, lhs_ref, sem):
    k = pl.program_id(0)
    j = pl.program_id(1)

    def fetch(tok_ref, r0, n):
        for r in range(n):
            _row_copy(hn_hbm, stage_ref, sem, tok_ref[0, r0 + r], r0 + r).start(priority=1)

    def wait_block():
        pltpu.make_async_copy(hn_hbm.at[pl.ds(0, TM_E)], stage_ref, sem).wait()

    @pl.when(j == 0)
    def _():
        @pl.when(k == 0)
        def _():
            def body(c, carry):
                fetch(tok0_ref, c * GU_ROWS, GU_ROWS)
                return carry
            lax.fori_loop(0, GU_NJ, body, 0)

        wait_block()

        def body(c, carry):
            rows = pl.ds(pl.multiple_of(c * 128, 128), 128)
            lo, hi = _unpack_bf16_pair(stage_ref[rows, :])
            lhs_ref[rows, 0:D // 2] = lo.astype(BF16)
            lhs_ref[rows, D // 2:D] = hi.astype(BF16)
            return carry
        lax.fori_loop(0, TM_E // 128, body, 0)

    fetch(tokn_ref, j * GU_ROWS, GU_ROWS)
    gu = jnp.dot(lhs_ref[...], w_ref[...].astype(BF16), preferred_element_type=F32) + b_ref[...]
    act = _swiglu_pairs(gu).astype(BF16)
    o_ref[...] = jnp.dot(act, p_ref[...], preferred_element_type=F32).astype(BF16)

    @pl.when((k == pl.num_programs(0) - 1) & (j == GU_NJ - 1))
    def _():
        wait_block()


def _moe_gate_up(hn_packed, tok_of_slot, w, b, blk_e, n_used):
    tn_out = GU_TN // 2
    tok3 = tok_of_slot.reshape(NB_E, 1, TM_E)
    grid_spec = pltpu.PrefetchScalarGridSpec(
        num_scalar_prefetch=1,
        grid=(n_used, GU_NJ),
        in_specs=[pl.BlockSpec((None, 1, TM_E), lambda k, j, e: (0, 0, 0), memory_space=pltpu.SMEM),
                  pl.BlockSpec((None, 1, TM_E), lambda k, j, e: (jnp.minimum(k + 1, NB_E - 1), 0, 0),
                               memory_space=pltpu.SMEM),
                  pl.BlockSpec(memory_space=pl.ANY),
                  pl.BlockSpec((None, D, GU_TN), lambda k, j, e: (e[k], 0, j)),
                  pl.BlockSpec((None, 1, GU_TN), lambda k, j, e: (e[k], 0, j)),
                  pl.BlockSpec((tn_out, tn_out), lambda k, j, e: (0, 0))],
        out_specs=pl.BlockSpec((TM_E, tn_out), lambda k, j, e: (k, j)),
        scratch_shapes=[pltpu.VMEM((TM_E, D // 2), U32), pltpu.VMEM((TM_E, D), BF16),
                        pltpu.SemaphoreType.DMA(())],
    )
    return pl.pallas_call(
        _gu_kernel,
        grid_spec=grid_spec,
        out_shape=jax.ShapeDtypeStruct((CAP_E, GU_NJ * tn_out), BF16),
        compiler_params=_cparams(2),
        name="moe_gate_up",
    )(blk_e, tok3, tok3, hn_packed, w, b.reshape(N_E, 1, 2 * D), jnp.asarray(_pair_perm(), BF16))


DN_TN = 512


def _down_kernel(e_ref, a_ref, w_ref, b_ref, o_ref):
    y = jnp.dot(a_ref[...], w_ref[...].astype(BF16), preferred_element_type=F32) + b_ref[...]
    o_ref[...] = _pack_bf16_pair(y[:, :DN_TN // 2], y[:, DN_TN // 2:])


def _moe_down(act, w, b, blk_e, n_used):
    grid_spec = pltpu.PrefetchScalarGridSpec(
        num_scalar_prefetch=1,
        grid=(n_used, D // DN_TN),
        in_specs=[pl.BlockSpec((TM_E, D), lambda k, j, e: (k, 0)),
                  pl.BlockSpec((None, D, DN_TN), lambda k, j, e: (e[k], 0, j)),
                  pl.BlockSpec((None, 1, DN_TN), lambda k, j, e: (e[k], 0, j))],
        out_specs=pl.BlockSpec((TM_E, DN_TN // 2), lambda k, j, e: (k, j)),
    )
    return pl.pallas_call(
        _down_kernel,
        grid_spec=grid_spec,
        out_shape=jax.ShapeDtypeStruct((CAP_E, D // 2), U32),
        compiler_params=_cparams(2),
        name="moe_down",
    )(blk_e, act, w, b.reshape(N_E, 1, D))


def _combine_kernel(slot_ref, slotn_ref, g_ref, h_ref, gp_ref, y_hbm, h2_ref, hn_ref, buf_ref, sems):
    i = pl.program_id(0)
    n = pl.num_programs(0)
    cur = i % 2

    def fetch(idx_ref, slot):
        def body(c, carry):
            for r in range(8):
                row = c * 8 + r
                for k in range(TOP_K):
                    _row_copy(y_hbm, buf_ref.at[slot, k], sems.at[slot],
                              idx_ref[0, TOP_K * row + k], row).start(priority=k % 2)
            return carry
        lax.fori_loop(0, GT // 8, body, 0)

    @pl.when(i == 0)
    def _():
        fetch(slot_ref, 0)

    @pl.when(i + 1 < n)
    def _():
        fetch(slotn_ref, 1 - cur)

    for k in range(TOP_K):
        pltpu.make_async_copy(y_hbm.at[pl.ds(0, GT)], buf_ref.at[cur, k], sems.at[cur]).wait()

    g = g_ref[...]
    tw = DN_TN // 2
    for c in range(D // DN_TN):
        lo_acc = h_ref[:, DN_TN * c:DN_TN * c + tw]
        hi_acc = h_ref[:, DN_TN * c + tw:DN_TN * (c + 1)]
        for k in range(TOP_K):
            lo, hi = _unpack_bf16_pair(buf_ref[cur, k, :, tw * c:tw * (c + 1)])
            lo_acc = lo_acc + g[:, k:k + 1] * lo
            hi_acc = hi_acc + g[:, k:k + 1] * hi
        h2_ref[:, DN_TN * c:DN_TN * c + tw] = lo_acc
        h2_ref[:, DN_TN * c + tw:DN_TN * (c + 1)] = hi_acc
    h2 = h2_ref[...]
    ms = jnp.mean(h2 * h2, axis=-1, keepdims=True)
    hn_ref[...] = (h2 * lax.rsqrt(ms + EPS) * gp_ref[...]).astype(BF16)


def _moe_combine(slot, gates, h1, g_ple, y_packed):
    n = T_ALL // GT
    slot3 = slot.reshape(n, 1, GT * TOP_K)
    return pl.pallas_call(
        _combine_kernel,
        grid=(n,),
        in_specs=[pl.BlockSpec((None, 1, GT * TOP_K), lambda i: (i, 0, 0), memory_space=pltpu.SMEM),
                  pl.BlockSpec((None, 1, GT * TOP_K), lambda i: (jnp.minimum(i + 1, n - 1), 0, 0),
                               memory_space=pltpu.SMEM),
                  pl.BlockSpec((GT, 128), lambda i: (i, 0)),
                  pl.BlockSpec((GT, D), lambda i: (i, 0)),
                  pl.BlockSpec((1, D), lambda i: (0, 0)),
                  pl.BlockSpec(memory_space=pl.ANY)],
        out_specs=[pl.BlockSpec((GT, D), lambda i: (i, 0)),
                   pl.BlockSpec((GT, D), lambda i: (i, 0))],
        out_shape=[jax.ShapeDtypeStruct((T_ALL, D), F32),
                   jax.ShapeDtypeStruct((T_ALL, D), BF16)],
        scratch_shapes=[pltpu.VMEM((2, TOP_K, GT, D // 2), U32), pltpu.SemaphoreType.DMA((2,))],
        compiler_params=_cparams(1),
        name="moe_combine",
    )(slot3, slot3, gates, h1, g_ple.reshape(1, D), y_packed)


def _routing_tables(top_i, rank, counts):
    flat_e = top_i.reshape(-1)
    nblk = (counts + TM_E - 1) // TM_E
    blk_end = jnp.cumsum(nblk)
    blk_start = blk_end - nblk
    slot = (blk_start[flat_e] * TM_E + rank.reshape(-1)).astype(I32)
    n_used = blk_end[-1].astype(I32)
    ks = jnp.arange(NB_E, dtype=I32)
    blk_e = jnp.minimum(jnp.searchsorted(blk_end, ks, side="right"), N_E - 1).astype(I32)
    flat_t = jnp.repeat(jnp.arange(T_ALL, dtype=I32), TOP_K)
    tok_of_slot = jnp.zeros((CAP_E,), I32).at[slot].set(flat_t, unique_indices=True)
    return slot, tok_of_slot, blk_e, n_used


def kernel(x_prompt, x_sample, cache_k, cache_v, state_conv, p_prompt, p_sample, g_mix, w_in, b_in, conv_w, sinks, w_out, b_out, g_ffn, w_router, b_router, w_gate_up, b_gate_up, w_down, b_down, g_ple, w_ple_gate, w_ple, g_final):
    xp = x_prompt[0]
    xs = x_sample.reshape(T_S, D)
    p_all = jnp.concatenate([p_prompt[0, 0], p_sample[0].reshape(T_S, PLE)], axis=0)

    u = _rms_norm_stacked(xp, xs, g_mix[0])
    b_in2 = b_in[0].reshape(1, N_IN)
    z32 = _dense(_mm_bias_kernel, u, w_in[0], [b_in2],
                 [pl.BlockSpec((1, CB), lambda i, j: (0, _z32_col(j)))], 1024, CB, "in_proj_f32",
                 col_map=_z32_col, n_tiles=Z32_TILES)
    z16 = _dense(_mm_bias_kernel, u, w_in[0], [b_in2],
                 [pl.BlockSpec((1, CB), lambda i, j: (0, _z16_col(j)))], 1024, CB, "in_proj_bf16",
                 col_map=_z16_col, n_tiles=Z16_TILES, out_dtype=BF16)

    y_p, krot_p, y_s, krot_s = _attention(z32, z16, cache_k[0], cache_v[0], sinks[0])

    merged, cx = _merge(z32, z16, y_p, y_s, state_conv[0], conv_w[0])
    n_p = T_P // 1024
    h1 = _dense(functools.partial(_mm_bias_res_kernel, n_p=n_p), merged, w_out[0],
                [b_out[0].reshape(1, D), xp, xs],
                [pl.BlockSpec((1, 512), lambda i, j: (0, j)),
                 pl.BlockSpec((1024, 512), lambda i, j: (jnp.minimum(i, n_p - 1), j)),
                 pl.BlockSpec((1024, 512), lambda i, j: (jnp.maximum(i - n_p, 0), j))],
                1024, 512, "out_proj")

    hn_packed, top_i, top_g, rank, counts = _router(h1, g_ffn[0], w_router[0], b_router[0])
    slot, tok_of_slot, blk_e, n_used = _routing_tables(
        top_i[:, :TOP_K], rank[:, :TOP_K], counts[0, :N_E])
    act = _moe_gate_up(hn_packed, tok_of_slot, w_gate_up[0], b_gate_up[0], blk_e, n_used)
    y_packed = _moe_down(act, w_down[0], b_down[0], blk_e, n_used)
    h2, hn2 = _moe_combine(slot, top_g, h1, g_ple[0], y_packed)

    h3 = _dense(_mm_ple_kernel, hn2, w_ple_gate[0], [h2, p_all, w_ple[0]],
                [pl.BlockSpec((1024, 512), lambda i, j: (i, j)),
                 pl.BlockSpec((1024, PLE), lambda i, j: (i, 0)),
                 pl.BlockSpec((PLE, 512), lambda i, j: (0, j))], 1024, 512, "ple")

    y_prompt = _rms_norm(h3, g_final, F32, 0, T_P).reshape(1, T_P, D)
    y_sample = _rms_norm(h3, g_final, F32, T_P, T_S).reshape(N_SEQ, S_LEN, D)

    w_p = min(WIN, T_P)
    k_win_p = krot_p[T_P - w_p:].reshape(1, 1, w_p, N_KV, HD)
    v_win_p = z32[T_P - w_p:T_P, Z32_V * CB:Z32_V * CB + KV_W].reshape(1, 1, w_p, N_KV, HD)
    conv_p = cx[MT - 2:MT].reshape(1, 1, 2, D)
    k_new = krot_s.reshape(N_SEQ, S_LEN, N_KV, HD)
    v_new = z32[T_P:, Z32_V * CB:Z32_V * CB + KV_W].reshape(N_SEQ, S_LEN, N_KV, HD)
    k_win_s = jnp.concatenate([cache_k[0][:, S_LEN:], k_new], axis=1)[None]
    v_win_s = jnp.concatenate([cache_v[0][:, S_LEN:], v_new], axis=1)[None]
    conv_s = cx[MT:].reshape(N_SEQ, S_LEN, D)[:, S_LEN - 2:][None]
    return (y_prompt, y_sample, k_win_p, v_win_p, conv_p, k_win_s, v_win_s, conv_s)
```
